```python
import math
import jax, jax.numpy as jnp
from jax import lax
import numpy as np

D_MODEL = 1024
BATCH = 8
SEQ = 4096
DEPTH = 4

HEAD_DIM = 64
N_FOX_HEADS = 8
N_SB_HEADS = 8
N_MOBA_HEADS = 8
N_MLA_HEADS = 8
Q_BLOCK = 128
MOBA_BLOCK = 256
MOBA_TOPK = 3
MOBA_Q_CHUNK = 32
MLA_Q_LORA = 256
MLA_KV_LORA = 128
MLA_NOPE_DIM = 64
MLA_ROPE_DIM = 32
MLA_V_DIM = 64
MLA_QK_DIM = MLA_NOPE_DIM + MLA_ROPE_DIM
ROPE_BASE = 10000.0
REL_BUCKETS = 32
REL_MAX_EXACT = 16
REL_MAX_DISTANCE = 128
D_FF = ((8 * D_MODEL + 3 * 256 - 1) // (3 * 256)) * 256
RMS_EPS = 1e-6
N_EVEN = (DEPTH + 1) // 2
N_ODD = DEPTH // 2
FOX_W = N_FOX_HEADS * HEAD_DIM
SB_W = N_SB_HEADS * HEAD_DIM
MOBA_W = N_MOBA_HEADS * HEAD_DIM
EVEN_WIDTHS = (FOX_W, FOX_W, FOX_W, N_FOX_HEADS, SB_W, SB_W, SB_W)
ODD_WIDTHS = (MOBA_W, MOBA_W, MOBA_W, MLA_Q_LORA, MLA_KV_LORA, MLA_ROPE_DIM)
EVEN_IN = sum(EVEN_WIDTHS)
ODD_IN = sum(ODD_WIDTHS)
EVEN_MIX = FOX_W + SB_W
ODD_MIX = MOBA_W + N_MLA_HEADS * MLA_V_DIM

kernel_name = "hybrid_fox_stickbreak_moba_mla_trunk"


def _rms_norm(t, gain):
    tf = t.astype(jnp.float32)
    tf = tf * lax.rsqrt(jnp.mean(tf * tf, axis=-1, keepdims=True) + RMS_EPS)
    return (tf * gain.astype(jnp.float32)).astype(t.dtype)


def _split(t, widths):
    cuts = [int(c) for c in np.cumsum(widths)[:-1]]
    return jnp.split(t, cuts, axis=-1)


def _split_heads(t, n_heads):
    b, s, _ = t.shape
    return t.reshape(b, s, n_heads, -1).transpose(0, 2, 1, 3)


def _merge_heads(t):
    b, h, s, d = t.shape
    return t.transpose(0, 2, 1, 3).reshape(b, s, h * d)


def _to_blocks(t, blk):
    b, h, s = t.shape[:3]
    t = t.reshape((b, h, s // blk, blk) + t.shape[3:])
    return jnp.moveaxis(t, 2, 0)


def _from_blocks(o):
    nb, b, h, blk, d = o.shape
    return jnp.moveaxis(o, 0, 2).reshape(b, h, nb * blk, d)


def _causal_softmax_attention(q, k, v, log_decay=None):
    s_len = q.shape[2]
    scale = q.shape[-1] ** -0.5
    kpos = jnp.arange(s_len, dtype=jnp.int32)
    starts = jnp.arange(s_len // Q_BLOCK, dtype=jnp.int32) * Q_BLOCK
    xs = (_to_blocks(q, Q_BLOCK), starts)
    if log_decay is not None:
        xs = xs + (_to_blocks(log_decay, Q_BLOCK),)

    def step(args):
        qb, start = args[0], args[1]
        logits = jnp.einsum("bhqd,bhkd->bhqk", qb, k).astype(jnp.float32) * scale
        if log_decay is not None:
            logits = logits + args[2][..., None] - log_decay[:, :, None, :]
        qpos = start + jnp.arange(Q_BLOCK, dtype=jnp.int32)
        causal = kpos[None, :] <= qpos[:, None]
        p = jax.nn.softmax(jnp.where(causal, logits, -jnp.inf), axis=-1)
        return jnp.einsum("bhqk,bhkd->bhqd", p.astype(v.dtype), v)

    return _from_blocks(lax.map(step, xs))


def _stick_breaking_attention(q, k, v):
    s_len = q.shape[2]
    scale = q.shape[-1] ** -0.5
    kpos = jnp.arange(s_len, dtype=jnp.int32)
    starts = jnp.arange(s_len // Q_BLOCK, dtype=jnp.int32) * Q_BLOCK

    def step(args):
        qb, start = args
        z = jnp.einsum("bhqd,bhkd->bhqk", qb, k).astype(jnp.float32) * scale
        qpos = start + jnp.arange(Q_BLOCK, dtype=jnp.int32)
        strict = kpos[None, :] < qpos[:, None]
        log_1m = jnp.where(strict, jax.nn.log_sigmoid(-z), 0.0)
        between = lax.cumsum(log_1m, axis=3, reverse=True) - log_1m
        log_w = jnp.where(strict, jax.nn.log_sigmoid(z) + between, -jnp.inf)
        return jnp.einsum("bhqk,bhkd->bhqd", jnp.exp(log_w).astype(v.dtype), v)

    return _from_blocks(lax.map(step, (_to_blocks(q, Q_BLOCK), starts)))


def _t5_bucket(rel):
    n = jnp.maximum(rel, 0)
    nf = jnp.maximum(n, 1).astype(jnp.float32)
    large = REL_MAX_EXACT + (jnp.log(nf / REL_MAX_EXACT)
                             / math.log(REL_MAX_DISTANCE / REL_MAX_EXACT)
                             * (REL_BUCKETS - REL_MAX_EXACT)).astype(jnp.int32)
    large = jnp.minimum(large, REL_BUCKETS - 1)
    return jnp.where(n < REL_MAX_EXACT, n, large)


def _moba_attention(q, k, v, rel_bias):
    b, h, s_len, d = q.shape
    scale = d ** -0.5
    s_pad = -(-s_len // MOBA_BLOCK) * MOBA_BLOCK
    pad = ((0, 0), (0, 0), (0, s_pad - s_len), (0, 0))
    q, k, v = (jnp.pad(t, pad) for t in (q, k, v))
    n_kb = s_pad // MOBA_BLOCK
    top = min(MOBA_TOPK, n_kb)
    kb = k.reshape(b, h, n_kb, MOBA_BLOCK, d)
    vb = v.reshape(b, h, n_kb, MOBA_BLOCK, d)
    k_mean = jnp.mean(kb.astype(jnp.float32), axis=3).astype(q.dtype)
    bi = jnp.arange(b)[:, None, None, None]
    hi = jnp.arange(h)[None, :, None, None]
    offs = jnp.arange(MOBA_BLOCK, dtype=jnp.int32)
    blk_ids = jnp.arange(n_kb, dtype=jnp.int32)
    starts = jnp.arange(s_pad // MOBA_Q_CHUNK, dtype=jnp.int32) * MOBA_Q_CHUNK
    n_sel = top * MOBA_BLOCK

    def step(args):
        qc, start = args
        cur = start // MOBA_BLOCK
        qpos = start + jnp.arange(MOBA_Q_CHUNK, dtype=jnp.int32)
        gate = jnp.einsum("bhqd,bhnd->bhqn", qc, k_mean).astype(jnp.float32)
        gate = jnp.where(blk_ids < cur, gate, -jnp.inf)
        _, sel = lax.top_k(gate, top)
        valid = sel < cur
        k_sel = kb[bi, hi, sel]
        v_sel = vb[bi, hi, sel]
        kpos_sel = sel[..., None] * MOBA_BLOCK + offs
        bias_sel = rel_bias[hi[..., None], _t5_bucket(qpos[:, None, None] - kpos_sel)]
        s_sel = (jnp.einsum("bhqd,bhqrkd->bhqrk", qc, k_sel).astype(jnp.float32) * scale
                 + bias_sel.astype(jnp.float32))
        s_sel = jnp.where(valid[..., None], s_sel, -jnp.inf)
        k_own = lax.dynamic_index_in_dim(kb, cur, axis=2, keepdims=False)
        v_own = lax.dynamic_index_in_dim(vb, cur, axis=2, keepdims=False)
        kpos_own = cur * MOBA_BLOCK + offs
        bias_own = rel_bias[:, _t5_bucket(qpos[:, None] - kpos_own[None, :])]
        s_own = (jnp.einsum("bhqd,bhkd->bhqk", qc, k_own).astype(jnp.float32) * scale
                 + bias_own.astype(jnp.float32))
        s_own = jnp.where(kpos_own[None, :] <= qpos[:, None], s_own, -jnp.inf)
        logits = jnp.concatenate([s_sel.reshape(b, h, MOBA_Q_CHUNK, n_sel), s_own], axis=-1)
        p = jax.nn.softmax(logits, axis=-1).astype(v.dtype)
        p_sel = p[..., :n_sel].reshape(b, h, MOBA_Q_CHUNK, top, MOBA_BLOCK)
        p_own = p[..., n_sel:]
        return (jnp.einsum("bhqrk,bhqrkd->bhqd", p_sel, v_sel)
                + jnp.einsum("bhqk,bhkd->bhqd", p_own, v_own))

    out = _from_blocks(lax.map(step, (_to_blocks(q, MOBA_Q_CHUNK), starts)))
    return out[:, :, :s_len]


def _rope_tail(t):
    s_len = t.shape[2]
    half = MLA_ROPE_DIM // 2
    inv_freq = ROPE_BASE ** (-jnp.arange(half, dtype=jnp.float32) / half)
    ang = jnp.arange(s_len, dtype=jnp.float32)[:, None] * inv_freq[None, :]
    cos, sin = jnp.cos(ang), jnp.sin(ang)
    nope, x1, x2 = _split(t, (MLA_NOPE_DIM, half, half))
    x1f, x2f = x1.astype(jnp.float32), x2.astype(jnp.float32)
    rot = jnp.concatenate([x1f * cos - x2f * sin, x1f * sin + x2f * cos], axis=-1)
    return jnp.concatenate([nope, rot.astype(t.dtype)], axis=-1)


def _even_mixer(h, w_in, forget_bias, fox_q_norm, fox_k_norm):
    qa, ka, va, fa, qb, kb, vb = _split(h @ w_in, EVEN_WIDTHS)
    qa = _rms_norm(_split_heads(qa, N_FOX_HEADS), fox_q_norm)
    ka = _rms_norm(_split_heads(ka, N_FOX_HEADS), fox_k_norm)
    log_f = jax.nn.log_sigmoid(fa.astype(jnp.float32) + forget_bias.astype(jnp.float32))
    log_decay = lax.cumsum(log_f.transpose(0, 2, 1), axis=2)
    out_a = _causal_softmax_attention(qa, ka, _split_heads(va, N_FOX_HEADS), log_decay)
    out_b = _stick_breaking_attention(_split_heads(qb, N_SB_HEADS), _split_heads(kb, N_SB_HEADS),
                                      _split_heads(vb, N_SB_HEADS))
    return jnp.concatenate([_merge_heads(out_a), _merge_heads(out_b)], axis=-1)


def _odd_mixer(h, w_in, moba_q_norm, moba_k_norm, q_a_norm, w_q_b, kv_a_norm, w_kv_b,
               mla_q_norm, mla_k_norm, rel_bias):
    qc, kc, vc, q_lat, kv_lat, k_rope = _split(h @ w_in, ODD_WIDTHS)
    qc = _rms_norm(_split_heads(qc, N_MOBA_HEADS), moba_q_norm)
    kc = _rms_norm(_split_heads(kc, N_MOBA_HEADS), moba_k_norm)
    out_c = _moba_attention(qc, kc, _split_heads(vc, N_MOBA_HEADS), rel_bias)
    qd = _split_heads(_rms_norm(q_lat, q_a_norm) @ w_q_b, N_MLA_HEADS)
    kvd = _split_heads(_rms_norm(kv_lat, kv_a_norm) @ w_kv_b, N_MLA_HEADS)
    k_nope, vd = _split(kvd, (MLA_NOPE_DIM, MLA_V_DIM))
    b, _, s_len, _ = k_nope.shape
    k_r = jnp.broadcast_to(k_rope[:, None], (b, N_MLA_HEADS, s_len, MLA_ROPE_DIM))
    kd = jnp.concatenate([k_nope, k_r], axis=-1)
    qd = _rope_tail(_rms_norm(qd, mla_q_norm))
    kd = _rope_tail(_rms_norm(kd, mla_k_norm))
    out_d = _causal_softmax_attention(qd, kd, vd)
    return jnp.concatenate([_merge_heads(out_c), _merge_heads(out_d)], axis=-1)


def _swiglu(h, w_gate_up, w_down):
    g, u = jnp.split(h @ w_gate_up, 2, axis=-1)
    return (jax.nn.silu(g) * u) @ w_down


def setup_inputs(seed: int = 0) -> dict:
    key = jax.random.key(seed)
    ks = jax.random.split(key, 22)

    def nrm(k, shape, scale):
        return scale * jax.random.normal(k, shape, jnp.float32)

    def gain(k, shape):
        return 1.0 + 0.02 * jax.random.normal(k, shape, jnp.float32)

    return {
        "x": jax.random.normal(ks[0], (BATCH, SEQ, D_MODEL), jnp.float32),
        "ffn_norm": gain(ks[1], (DEPTH, D_MODEL)),
        "ffn_w_gate_up": nrm(ks[2], (DEPTH, D_MODEL, 2 * D_FF), D_MODEL ** -0.5),
        "ffn_w_down": nrm(ks[3], (DEPTH, D_FF, D_MODEL), D_FF ** -0.5),
        "rel_bias": nrm(ks[4], (N_MOBA_HEADS, REL_BUCKETS), 0.5),
        "ev_norm": gain(ks[5], (N_EVEN, D_MODEL)),
        "ev_w_in": nrm(ks[6], (N_EVEN, D_MODEL, EVEN_IN), D_MODEL ** -0.5),
        "ev_forget_bias": jax.random.uniform(ks[7], (N_EVEN, N_FOX_HEADS), jnp.float32, 1.0, 4.0),
        "ev_fox_q_norm": gain(ks[8], (N_EVEN, HEAD_DIM)),
        "ev_fox_k_norm": gain(ks[9], (N_EVEN, HEAD_DIM)),
        "ev_w_out": nrm(ks[10], (N_EVEN, EVEN_MIX, D_MODEL), EVEN_MIX ** -0.5),
        "od_norm": gain(ks[11], (N_ODD, D_MODEL)),
        "od_w_in": nrm(ks[12], (N_ODD, D_MODEL, ODD_IN), D_MODEL ** -0.5),
        "od_moba_q_norm": gain(ks[13], (N_ODD, HEAD_DIM)),
        "od_moba_k_norm": gain(ks[14], (N_ODD, HEAD_DIM)),
        "od_mla_q_a_norm": gain(ks[15], (N_ODD, MLA_Q_LORA)),
        "od_mla_w_q_b": nrm(ks[16], (N_ODD, MLA_Q_LORA, N_MLA_HEADS * MLA_QK_DIM), MLA_Q_LORA ** -0.5),
        "od_mla_kv_a_norm": gain(ks[17], (N_ODD, MLA_KV_LORA)),
        "od_mla_w_kv_b": nrm(ks[18], (N_ODD, MLA_KV_LORA, N_MLA_HEADS * (MLA_NOPE_DIM + MLA_V_DIM)),
                             MLA_KV_LORA ** -0.5),
        "od_mla_q_norm": gain(ks[19], (N_ODD, MLA_QK_DIM)),
        "od_mla_k_norm": gain(ks[20], (N_ODD, MLA_QK_DIM)),
        "od_w_out": nrm(ks[21], (N_ODD, ODD_MIX, D_MODEL), ODD_MIX ** -0.5),
    }


def reference(x, ffn_norm, ffn_w_gate_up, ffn_w_down, rel_bias, ev_norm, ev_w_in,
              ev_forget_bias, ev_fox_q_norm, ev_fox_k_norm, ev_w_out, od_norm, od_w_in,
              od_moba_q_norm, od_moba_k_norm, od_mla_q_a_norm, od_mla_w_q_b,
              od_mla_kv_a_norm, od_mla_w_kv_b, od_mla_q_norm, od_mla_k_norm, od_w_out):
    for layer in range(DEPTH):
        i = layer // 2
        if layer % 2 == 0:
            h = _rms_norm(x, ev_norm[i])
            mixed = _even_mixer(h, ev_w_in[i], ev_forget_bias[i], ev_fox_q_norm[i], ev_fox_k_norm[i])
            x = x + mixed @ ev_w_out[i]
        else:
            h = _rms_norm(x, od_norm[i])
            mixed = _odd_mixer(h, od_w_in[i], od_moba_q_norm[i], od_moba_k_norm[i],
                               od_mla_q_a_norm[i], od_mla_w_q_b[i], od_mla_kv_a_norm[i],
                               od_mla_w_kv_b[i], od_mla_q_norm[i], od_mla_k_norm[i], rel_bias)
            x = x + mixed @ od_w_out[i]
        h = _rms_norm(x, ffn_norm[layer])
        x = x + _swiglu(h, ffn_w_gate_up[layer], ffn_w_down[layer])
    return x
```

```python
import functools
import math

import jax
import jax.numpy as jnp
from jax import lax
from jax.experimental import pallas as pl
from jax.experimental.pallas import tpu as pltpu

F32 = jnp.float32
BF16 = jnp.bfloat16

HEAD_DIM = 64
N_HEADS = 8
N_PAIRS = N_HEADS // 2
LANES = 128
MOBA_BLOCK = 256
MOBA_TOPK = 3
MLA_Q_LORA = 256
MLA_KV_LORA = 128
MLA_NOPE = 64
MLA_ROPE = 32
MLA_QK = MLA_NOPE + MLA_ROPE
ROPE_BASE = 10000.0
REL_BUCKETS = 32
REL_MAX_EXACT = 16
REL_MAX_DISTANCE = 128
RMS_EPS = 1e-6
TQ = 256
TK = 256
PREP_ROWS = 512
DEAD_LOG = -110.0
VMEM_LIMIT = 56 * 1024 * 1024
HIGHEST = lax.Precision.HIGHEST


def _cparams(*sem):
    return pltpu.CompilerParams(dimension_semantics=sem, vmem_limit_bytes=VMEM_LIMIT)


def _dot(a, b, **kw):
    return jnp.dot(a, b, preferred_element_type=F32, **kw)


def _dot_nt(a, b, **kw):
    return lax.dot_general(a, b, (((1,), (1,)), ((), ())), preferred_element_type=F32, **kw)


def _log_sigmoid_parts(z):
    sp = jnp.log1p(jnp.exp(-jnp.abs(z)))
    return jnp.minimum(z, 0.0) - sp, -jnp.maximum(z, 0.0) - sp


def _lane_iota():
    return lax.broadcasted_iota(jnp.int32, (1, LANES), 1)


def _pair_rms(x, gain, first):
    sq = x * x
    s0 = jnp.sum(jnp.where(first, sq, 0.0), axis=1, keepdims=True)
    s1 = jnp.sum(jnp.where(first, 0.0, sq), axis=1, keepdims=True)
    ms = jnp.where(first, s0, s1) * (1.0 / HEAD_DIM)
    return x * lax.rsqrt(ms + RMS_EPS) * gain


def _norm_matmul_kernel(x_ref, g_ref, w_ref, o_ref, *, n_chunk):
    x = x_ref[...]
    ms = jnp.mean(x * x, axis=-1, keepdims=True)
    h = (x * lax.rsqrt(ms + RMS_EPS) * g_ref[...]).astype(BF16)
    n = o_ref.shape[1]
    for c0 in range(0, n, n_chunk):
        c1 = min(n, c0 + n_chunk)
        o_ref[:, c0:c1] = _dot(h, w_ref[:, c0:c1])


def _norm_matmul(xw, col_blk, k, gain, w, tm=256, n_chunk=512):
    m = xw.shape[0]
    n = w.shape[1]
    return pl.pallas_call(
        functools.partial(_norm_matmul_kernel, n_chunk=n_chunk),
        grid=(m // tm,),
        in_specs=[
            pl.BlockSpec((tm, k), lambda i: (i, col_blk)),
            pl.BlockSpec((1, k), lambda i: (0, 0)),
            pl.BlockSpec((k, n), lambda i: (0, 0)),
        ],
        out_specs=pl.BlockSpec((tm, n), lambda i: (i, 0)),
        out_shape=jax.ShapeDtypeStruct((m, n), F32),
        compiler_params=_cparams("parallel"),
        name="norm_matmul",
    )(xw, gain.reshape(1, k), w)


def _proj_res_kernel(a_ref, b_ref, wa_ref, wb_ref, r_ref, o_ref):
    o_ref[...] = r_ref[...] + (_dot(a_ref[...], wa_ref[...]) + _dot(b_ref[...], wb_ref[...]))


def _proj_residual(a, b, w_out, res, tm=512):
    m, d = res.shape
    ka = a.shape[1]
    w = w_out.astype(BF16)
    return pl.pallas_call(
        _proj_res_kernel,
        grid=(m // tm,),
        in_specs=[
            pl.BlockSpec((tm, ka), lambda i: (i, 0)),
            pl.BlockSpec((tm, ka), lambda i: (i, 0)),
            pl.BlockSpec((ka, d), lambda i: (0, 0)),
            pl.BlockSpec((ka, d), lambda i: (1, 0)),
            pl.BlockSpec((tm, d), lambda i: (i, 0)),
        ],
        out_specs=pl.BlockSpec((tm, d), lambda i: (i, 0)),
        out_shape=jax.ShapeDtypeStruct((m, d), F32),
        compiler_params=_cparams("parallel"),
        name="proj_residual",
    )(a, b, w, w, res)


def _ffn_kernel(x_ref, g_ref, wg_ref, wu_ref, wd_ref, o_ref, h_s, acc_s):
    c = pl.program_id(1)

    @pl.when(c == 0)
    def _():
        x = x_ref[...]
        ms = jnp.mean(x * x, axis=-1, keepdims=True)
        h_s[...] = (x * lax.rsqrt(ms + RMS_EPS) * g_ref[...]).astype(BF16)
        acc_s[...] = jnp.zeros_like(acc_s)

    h = h_s[...]
    g = _dot(h, wg_ref[...])
    u = _dot(h, wu_ref[...])
    act = (g * jax.nn.sigmoid(g) * u).astype(BF16)
    acc_s[...] += _dot(act, wd_ref[...])

    @pl.when(c == pl.num_programs(1) - 1)
    def _():
        o_ref[...] = x_ref[...] + acc_s[...]


def _ffn(x, gain, w_gate_up, w_down, tm=512, n_ff_chunks=2):
    m, d = x.shape
    d_ff = w_down.shape[0]
    tf = d_ff // n_ff_chunks
    assert tf * n_ff_chunks == d_ff and tf % LANES == 0
    wgu = w_gate_up.astype(BF16)
    wd = w_down.astype(BF16)
    return pl.pallas_call(
        _ffn_kernel,
        grid=(m // tm, n_ff_chunks),
        in_specs=[
            pl.BlockSpec((tm, d), lambda i, c: (i, 0)),
            pl.BlockSpec((1, d), lambda i, c: (0, 0)),
            pl.BlockSpec((d, tf), lambda i, c: (0, c)),
            pl.BlockSpec((d, tf), lambda i, c: (0, n_ff_chunks + c)),
            pl.BlockSpec((tf, d), lambda i, c: (c, 0)),
        ],
        out_specs=pl.BlockSpec((tm, d), lambda i, c: (i, 0)),
        out_shape=jax.ShapeDtypeStruct((m, d), F32),
        scratch_shapes=[pltpu.VMEM((tm, d), BF16), pltpu.VMEM((tm, d), F32)],
        compiler_params=_cparams("parallel", "arbitrary"),
        name="swiglu_ffn",
    )(x, gain.reshape(1, d), wgu, wgu, wd)


def _decay_kernel(f_ref, b_ref, ccol_ref, crow_ref, carry_s):
    @pl.when(pl.program_id(1) == 0)
    def _():
        carry_s[...] = jnp.zeros_like(carry_s)

    ts = f_ref.shape[0]
    logf, _ = _log_sigmoid_parts(f_ref[...] + b_ref[...])
    r = lax.broadcasted_iota(jnp.int32, (ts, ts), 0)
    c = lax.broadcasted_iota(jnp.int32, (ts, ts), 1)
    tri = (c <= r).astype(F32)
    cs = _dot(tri, logf, precision=HIGHEST) + carry_s[...]
    carry_s[...] = cs[ts - 1:ts, :]
    ccol_ref[...] = cs[:, 0:N_HEADS]
    crow_ref[0] = cs.T[0:N_HEADS, :]


def _log_decay(p, col_blk, bias, b, s, ts=256):
    ns = s // ts
    bias128 = jnp.zeros((1, LANES), F32).at[0, :N_HEADS].set(bias)
    return pl.pallas_call(
        _decay_kernel,
        grid=(b, ns),
        in_specs=[
            pl.BlockSpec((ts, LANES), lambda bi, si: (bi * ns + si, col_blk)),
            pl.BlockSpec((1, LANES), lambda bi, si: (0, 0)),
        ],
        out_specs=[
            pl.BlockSpec((ts, N_HEADS), lambda bi, si: (bi * ns + si, 0)),
            pl.BlockSpec((1, N_HEADS, ts), lambda bi, si: (bi, 0, si)),
        ],
        out_shape=[
            jax.ShapeDtypeStruct((b * s, N_HEADS), F32),
            jax.ShapeDtypeStruct((b, N_HEADS, s), F32),
        ],
        scratch_shapes=[pltpu.VMEM((1, LANES), F32)],
        compiler_params=_cparams("parallel", "arbitrary"),
        name="fox_log_decay",
    )(p, bias128)


def _softmax_step(s, m, l, acc, v_blk):
    m_new = jnp.maximum(m, jnp.max(s, axis=1, keepdims=True))
    alpha = jnp.exp(m - m_new)
    pr = jnp.exp(s - m_new)
    l = alpha * l + jnp.sum(pr, axis=1, keepdims=True)
    acc = alpha * acc + _dot(pr.astype(BF16), v_blk)
    return m_new, l, acc


def _causal_tile():
    r = lax.broadcasted_iota(jnp.int32, (TQ, TK), 0)
    c = lax.broadcasted_iota(jnp.int32, (TQ, TK), 1)
    return r, c


def _softmax_init():
    return (jnp.full((TQ, 1), -jnp.inf, F32), jnp.zeros((TQ, 1), F32), jnp.zeros((TQ, LANES), F32))


def _prep_loop(n_rows, fn):
    def body(i, carry):
        fn(pl.multiple_of(i * PREP_ROWS, PREP_ROWS))
        return carry

    lax.fori_loop(0, n_rows // PREP_ROWS, body, 0)


def _pair_specs(nq, s, qcol, kcol, vcol):
    return [
        pl.BlockSpec((TQ, LANES), lambda b, hp, qi: (b * nq + qi, qcol + hp)),
        pl.BlockSpec((s, LANES), lambda b, hp, qi: (b, kcol + hp)),
        pl.BlockSpec((s, LANES), lambda b, hp, qi: (b, vcol + hp)),
    ]


def _fox_kernel(q_ref, k_ref, v_ref, cc_ref, cr_ref, gq_ref, gk_ref, o_ref, kn_s, vb_s, *, seq):
    hp = pl.program_id(1)
    qi = pl.program_id(2)
    first = _lane_iota() < HEAD_DIM

    @pl.when(qi == 0)
    def _():
        def prep(r0):
            kn_s[pl.ds(r0, PREP_ROWS), :] = _pair_rms(k_ref[pl.ds(r0, PREP_ROWS), :], gk_ref[...], first).astype(BF16)
            vb_s[pl.ds(r0, PREP_ROWS), :] = v_ref[pl.ds(r0, PREP_ROWS), :].astype(BF16)

        _prep_loop(seq, prep)

    qn = _pair_rms(q_ref[...], gq_ref[...], first) * (HEAD_DIM ** -0.5)
    rows, cols = _causal_tile()
    causal = cols <= rows
    head_lane = lax.broadcasted_iota(jnp.int32, (1, N_HEADS), 1)
    cc = cc_ref[...]
    outs = []
    for j in range(2):
        h = 2 * hp + j
        qj = jnp.where(first if j == 0 else jnp.logical_not(first), qn, 0.0).astype(BF16)
        cq = jnp.sum(jnp.where(head_lane == h, cc, 0.0), axis=1, keepdims=True)

        def logits(kj, qj=qj, cq=cq, h=h):
            k0 = pl.multiple_of(kj * TK, TK)
            s = _dot_nt(qj, kn_s[pl.ds(k0, TK), :])
            return s + cq - cr_ref[0, pl.ds(h, 1), pl.ds(k0, TK)], k0

        s, k0 = logits(qi)
        carry = _softmax_step(jnp.where(causal, s, -jnp.inf), *_softmax_init(), vb_s[pl.ds(k0, TK), :])

        def body(i, carry, logits=logits):
            s, k0 = logits(qi - 1 - i)
            return _softmax_step(s, *carry, vb_s[pl.ds(k0, TK), :])

        _, l, acc = lax.fori_loop(0, qi, body, carry)
        outs.append(acc / l)
    o_ref[...] = jnp.where(first, outs[0], outs[1]).astype(o_ref.dtype)


def _fox_attention(p, ccol, crow, gq, gk, b, s):
    nq = s // TQ
    gq2 = jnp.tile(gq, 2).reshape(1, LANES)
    gk2 = jnp.tile(gk, 2).reshape(1, LANES)
    return pl.pallas_call(
        functools.partial(_fox_kernel, seq=s),
        grid=(b, N_PAIRS, nq),
        in_specs=_pair_specs(nq, s, 0, N_PAIRS, 2 * N_PAIRS) + [
            pl.BlockSpec((TQ, N_HEADS), lambda b_, hp, qi: (b_ * nq + qi, 0)),
            pl.BlockSpec((1, N_HEADS, s), lambda b_, hp, qi: (b_, 0, 0)),
            pl.BlockSpec((1, LANES), lambda b_, hp, qi: (0, 0)),
            pl.BlockSpec((1, LANES), lambda b_, hp, qi: (0, 0)),
        ],
        out_specs=pl.BlockSpec((TQ, LANES), lambda b_, hp, qi: (b_ * nq + qi, hp)),
        out_shape=jax.ShapeDtypeStruct((b * s, N_PAIRS * LANES), BF16),
        scratch_shapes=[pltpu.VMEM((s, LANES), BF16), pltpu.VMEM((s, LANES), BF16)],
        compiler_params=_cparams("parallel", "parallel", "arbitrary"),
        name="fox_attention",
    )(p, p, p, ccol, crow, gq2, gk2)


def _sb_kernel(q_ref, k_ref, v_ref, u_ref, o_ref, kb_s, vb_s, *, seq):
    qi = pl.program_id(2)
    first = _lane_iota() < HEAD_DIM

    @pl.when(qi == 0)
    def _():
        def prep(r0):
            kb_s[pl.ds(r0, PREP_ROWS), :] = k_ref[pl.ds(r0, PREP_ROWS), :].astype(BF16)
            vb_s[pl.ds(r0, PREP_ROWS), :] = v_ref[pl.ds(r0, PREP_ROWS), :].astype(BF16)

        _prep_loop(seq, prep)

    q = q_ref[...] * (HEAD_DIM ** -0.5)
    rows, cols = _causal_tile()
    strict = cols < rows
    outs = []
    for j in range(2):
        qj = jnp.where(first if j == 0 else jnp.logical_not(first), q, 0.0).astype(BF16)

        def chunk(kj, rsum, acc, diag, qj=qj):
            k0 = pl.multiple_of(kj * TK, TK)
            z = _dot_nt(qj, kb_s[pl.ds(k0, TK), :])
            log_b, log_1m = _log_sigmoid_parts(z)
            if diag:
                log_1m = jnp.where(strict, log_1m, 0.0)
            hi = log_1m.astype(BF16)
            lo = (log_1m - hi.astype(F32)).astype(BF16)
            between = _dot(hi, u_ref[...]) + _dot(lo, u_ref[...])
            w = jnp.exp(log_b + rsum + between)
            if diag:
                w = jnp.where(strict, w, 0.0)
            acc = acc + _dot(w.astype(BF16), vb_s[pl.ds(k0, TK), :])
            return rsum + jnp.sum(log_1m, axis=1, keepdims=True), acc

        rsum, acc = chunk(qi, jnp.zeros((TQ, 1), F32), jnp.zeros((TQ, LANES), F32), True)

        def cond(c):
            return jnp.logical_and(c[0] >= 0, jnp.max(c[1]) > DEAD_LOG)

        def body(c, chunk=chunk):
            rs, ac = chunk(c[0], c[1], c[2], False)
            return c[0] - 1, rs, ac

        _, _, acc = lax.while_loop(cond, body, (qi - 1, rsum, acc))
        outs.append(acc)
    o_ref[...] = jnp.where(first, outs[0], outs[1]).astype(o_ref.dtype)


def _sb_attention(p, qcol, b, s):
    nq = s // TQ
    r = lax.broadcasted_iota(jnp.int32, (TK, TK), 0)
    c = lax.broadcasted_iota(jnp.int32, (TK, TK), 1)
    later = (r > c).astype(BF16)
    return pl.pallas_call(
        functools.partial(_sb_kernel, seq=s),
        grid=(b, N_PAIRS, nq),
        in_specs=_pair_specs(nq, s, qcol, qcol + N_PAIRS, qcol + 2 * N_PAIRS) + [
            pl.BlockSpec((TK, TK), lambda b_, hp, qi: (0, 0)),
        ],
        out_specs=pl.BlockSpec((TQ, LANES), lambda b_, hp, qi: (b_ * nq + qi, hp)),
        out_shape=jax.ShapeDtypeStruct((b * s, N_PAIRS * LANES), BF16),
        scratch_shapes=[pltpu.VMEM((s, LANES), BF16), pltpu.VMEM((s, LANES), BF16)],
        compiler_params=_cparams("parallel", "parallel", "arbitrary"),
        name="stick_breaking_attention",
    )(p, p, p, later)


def _t5_bucket(rel):
    n = jnp.maximum(rel, 0)
    nf = jnp.maximum(n, 1).astype(F32)
    large = REL_MAX_EXACT + (jnp.log(nf / REL_MAX_EXACT) / math.log(REL_MAX_DISTANCE / REL_MAX_EXACT)
                             * (REL_BUCKETS - REL_MAX_EXACT)).astype(jnp.int32)
    large = jnp.minimum(large, REL_BUCKETS - 1)
    return jnp.where(n < REL_MAX_EXACT, n, large)


def _bias_tile_kernel(rb_ref, own_ref, prev_ref):
    h = pl.program_id(0)
    t = lax.broadcasted_iota(jnp.int32, (MOBA_BLOCK, MOBA_BLOCK), 0)
    s = lax.broadcasted_iota(jnp.int32, (MOBA_BLOCK, MOBA_BLOCK), 1)
    for off, ref in ((0, own_ref), (MOBA_BLOCK, prev_ref)):
        bucket = _t5_bucket(t - s + off)
        out = jnp.zeros((MOBA_BLOCK, MOBA_BLOCK), F32)
        for bk in range(REL_BUCKETS):
            out = jnp.where(bucket == bk, rb_ref[h, bk], out)
        ref[0] = out


def _bias_tiles(rel_bias):
    shape = jax.ShapeDtypeStruct((N_HEADS, MOBA_BLOCK, MOBA_BLOCK), F32)
    spec = pl.BlockSpec((1, MOBA_BLOCK, MOBA_BLOCK), lambda h: (h, 0, 0))
    return pl.pallas_call(
        _bias_tile_kernel,
        grid=(N_HEADS,),
        in_specs=[pl.BlockSpec(memory_space=pltpu.SMEM)],
        out_specs=[spec, spec],
        out_shape=[shape, shape],
        compiler_params=_cparams("arbitrary"),
        name="moba_bias_tiles",
    )(rel_bias)


def _moba_kernel(rb_ref, q_ref, k_ref, v_ref, bown_ref, bprev_ref, gq_ref, gk_ref, o_ref,
                 kn_s, vb_s, km_s, *, seq):
    hp = pl.program_id(1)
    cur = pl.program_id(2)
    nb = seq // MOBA_BLOCK
    first = _lane_iota() < HEAD_DIM

    @pl.when(cur == 0)
    def _():
        def prep(n, carry):
            r0 = pl.multiple_of(n * MOBA_BLOCK, MOBA_BLOCK)
            kn = _pair_rms(k_ref[pl.ds(r0, MOBA_BLOCK), :], gk_ref[...], first)
            kn_s[pl.ds(r0, MOBA_BLOCK), :] = kn.astype(BF16)
            km_s[pl.ds(n, 1), :] = jnp.mean(kn, axis=0, keepdims=True)
            vb_s[pl.ds(r0, MOBA_BLOCK), :] = v_ref[pl.ds(r0, MOBA_BLOCK), :].astype(BF16)
            return carry

        lax.fori_loop(0, nb, prep, 0)

    qn = _pair_rms(q_ref[...], gq_ref[...], first) * (HEAD_DIM ** -0.5)
    rows, cols = _causal_tile()
    causal = cols <= rows
    blk = lax.broadcasted_iota(jnp.int32, (1, nb), 1)
    past = blk < cur
    own0 = pl.multiple_of(cur * MOBA_BLOCK, MOBA_BLOCK)
    outs = []
    for j in range(2):
        h = 2 * hp + j
        qf = jnp.where(first if j == 0 else jnp.logical_not(first), qn, 0.0)
        qj = qf.astype(BF16)
        gate = jnp.where(past, _dot_nt(qf, km_s[...], precision=HIGHEST), -jnp.inf)
        beaten = jnp.zeros((TQ, nb), F32)
        for n2 in range(nb):
            g2 = gate[:, n2:n2 + 1]
            wins = jnp.logical_or(g2 > gate, jnp.logical_and(g2 == gate, n2 < blk))
            beaten = beaten + jnp.where(wins, 1.0, 0.0)
        sel = jnp.where(jnp.logical_and(beaten < min(MOBA_TOPK, nb), past), 1.0, 0.0)

        s = _dot_nt(qj, kn_s[pl.ds(own0, MOBA_BLOCK), :]) + bown_ref[j]
        carry = _softmax_step(jnp.where(causal, s, -jnp.inf), *_softmax_init(), vb_s[pl.ds(own0, MOBA_BLOCK), :])
        far_bias = rb_ref[h, REL_BUCKETS - 1]

        def body(n, carry, qj=qj, sel=sel, far_bias=far_bias, j=j):
            k0 = pl.multiple_of(n * MOBA_BLOCK, MOBA_BLOCK)
            s = _dot_nt(qj, kn_s[pl.ds(k0, MOBA_BLOCK), :])
            s = s + jnp.where(n == cur - 1, bprev_ref[j], far_bias)
            chosen = jnp.sum(jnp.where(blk == n, sel, 0.0), axis=1, keepdims=True) > 0.5
            return _softmax_step(jnp.where(chosen, s, -jnp.inf), *carry, vb_s[pl.ds(k0, MOBA_BLOCK), :])

        _, l, acc = lax.fori_loop(0, cur, body, carry)
        outs.append(acc / l)
    o_ref[...] = jnp.where(first, outs[0], outs[1]).astype(o_ref.dtype)


def _moba_attention(p, rel_bias, bown, bprev, gq, gk, b, s):
    assert TQ == MOBA_BLOCK and TK == MOBA_BLOCK and s % MOBA_BLOCK == 0
    nb = s // MOBA_BLOCK
    gq2 = jnp.tile(gq, 2).reshape(1, LANES)
    gk2 = jnp.tile(gk, 2).reshape(1, LANES)
    bias_spec = pl.BlockSpec((2, MOBA_BLOCK, MOBA_BLOCK), lambda b_, hp, qi: (hp, 0, 0))
    return pl.pallas_call(
        functools.partial(_moba_kernel, seq=s),
        grid=(b, N_PAIRS, nb),
        in_specs=[pl.BlockSpec(memory_space=pltpu.SMEM)] + _pair_specs(nb, s, 0, N_PAIRS, 2 * N_PAIRS) + [
            bias_spec, bias_spec,
            pl.BlockSpec((1, LANES), lambda b_, hp, qi: (0, 0)),
            pl.BlockSpec((1, LANES), lambda b_, hp, qi: (0, 0)),
        ],
        out_specs=pl.BlockSpec((TQ, LANES), lambda b_, hp, qi: (b_ * nb + qi, hp)),
        out_shape=jax.ShapeDtypeStruct((b * s, N_PAIRS * LANES), BF16),
        scratch_shapes=[pltpu.VMEM((s, LANES), BF16), pltpu.VMEM((s, LANES), BF16),
                        pltpu.VMEM((nb, LANES), F32)],
        compiler_params=_cparams("parallel", "parallel", "arbitrary"),
        name="moba_attention",
    )(rel_bias, p, p, p, bown, bprev, gq2, gk2)


def _rope(x, cos, sin_lo, sin_hi):
    return x * cos + pltpu.roll(x, LANES - MLA_ROPE // 2, 1) * sin_lo + pltpu.roll(x, MLA_ROPE // 2, 1) * sin_hi


def _mla_norm_rope(x, gain, cos, sin_lo, sin_hi):
    ms = jnp.sum(x * x, axis=1, keepdims=True) * (1.0 / MLA_QK)
    return _rope(x * lax.rsqrt(ms + RMS_EPS) * gain, cos, sin_lo, sin_hi)


def _mla_kernel(q_ref, kv_ref, kr_ref, ck_ref, slk_ref, shk_ref, cq_ref, slq_ref, shq_ref, gq_ref, gk_ref,
                o_ref, kn_s, vb_s, *, seq):
    qi = pl.program_id(2)
    first = _lane_iota() < HEAD_DIM

    @pl.when(qi == 0)
    def _():
        def prep(r0):
            rs = pl.ds(r0, PREP_ROWS)
            for j in range(2):
                kv = kv_ref[rs, j * LANES:(j + 1) * LANES]
                k = jnp.where(first, kv, kr_ref[rs, :])
                kn_s[j, rs, :] = _mla_norm_rope(k, gk_ref[...], ck_ref[rs, :], slk_ref[rs, :], shk_ref[rs, :]).astype(BF16)
                vb_s[j, rs, :] = kv.astype(BF16)

        _prep_loop(seq, prep)

    rows, cols = _causal_tile()
    causal = cols <= rows
    scale = MLA_QK ** -0.5
    outs = []
    for j in range(2):
        qj = _mla_norm_rope(q_ref[:, j * LANES:(j + 1) * LANES], gq_ref[...],
                            cq_ref[...], slq_ref[...], shq_ref[...]).astype(BF16)

        def step(kj, carry, mask, qj=qj, j=j):
            k0 = pl.multiple_of(kj * TK, TK)
            s = _dot_nt(qj, kn_s[j, pl.ds(k0, TK), :]) * scale
            if mask:
                s = jnp.where(causal, s, -jnp.inf)
            return _softmax_step(s, *carry, vb_s[j, pl.ds(k0, TK), :])

        carry = step(qi, _softmax_init(), True)
        _, l, acc = lax.fori_loop(0, qi, lambda i, c, step=step: step(qi - 1 - i, c, False), carry)
        outs.append(acc / l)
    o_ref[...] = jnp.where(first, pltpu.roll(outs[0], HEAD_DIM, 1), outs[1]).astype(o_ref.dtype)


def _mla_attention(qd, kvd, p, kr_col, tables, gq, gk, b, s):
    nq = s // TQ
    cos, sin_lo, sin_hi = tables
    gq128 = jnp.zeros((1, LANES), F32).at[0, :MLA_QK].set(gq)
    gk128 = jnp.zeros((1, LANES), F32).at[0, :MLA_QK].set(gk)
    full = pl.BlockSpec((s, LANES), lambda b_, hp, qi: (0, 0))
    rowsq = pl.BlockSpec((TQ, LANES), lambda b_, hp, qi: (qi, 0))
    gain = pl.BlockSpec((1, LANES), lambda b_, hp, qi: (0, 0))
    return pl.pallas_call(
        functools.partial(_mla_kernel, seq=s),
        grid=(b, N_PAIRS, nq),
        in_specs=[
            pl.BlockSpec((TQ, 2 * LANES), lambda b_, hp, qi: (b_ * nq + qi, hp)),
            pl.BlockSpec((s, 2 * LANES), lambda b_, hp, qi: (b_, hp)),
            pl.BlockSpec((s, LANES), lambda b_, hp, qi: (b_, kr_col)),
            full, full, full, rowsq, rowsq, rowsq, gain, gain,
        ],
        out_specs=pl.BlockSpec((TQ, LANES), lambda b_, hp, qi: (b_ * nq + qi, hp)),
        out_shape=jax.ShapeDtypeStruct((b * s, N_PAIRS * LANES), BF16),
        scratch_shapes=[pltpu.VMEM((2, s, LANES), BF16), pltpu.VMEM((2, s, LANES), BF16)],
        compiler_params=_cparams("parallel", "parallel", "arbitrary"),
        name="mla_attention",
    )(qd, kvd, p, cos, sin_lo, sin_hi, cos, sin_lo, sin_hi, gq128, gk128)


def _rope_tables(s):
    half = MLA_ROPE // 2
    inv_freq = ROPE_BASE ** (-jnp.arange(half, dtype=F32) / half)
    ang = jnp.arange(s, dtype=F32)[:, None] * inv_freq[None, :]
    cos, sin = jnp.cos(ang), jnp.sin(ang)
    z = lambda w: jnp.zeros((s, w), F32)
    cos_t = jnp.concatenate([jnp.ones((s, MLA_NOPE), F32), cos, cos, z(LANES - MLA_QK)], axis=1)
    sin_lo = jnp.concatenate([z(MLA_NOPE), -sin, z(LANES - MLA_NOPE - half)], axis=1)
    sin_hi = jnp.concatenate([z(MLA_NOPE + half), sin, z(LANES - MLA_QK)], axis=1)
    return cos_t, sin_lo, sin_hi


def _even_layer(x, b, s, norm, w_in, forget_bias, gq, gk, w_out):
    d = x.shape[1]
    w = HEAD_DIM * N_HEADS
    cuts = [0, w, 2 * w, 3 * w, 3 * w + N_HEADS, 4 * w + N_HEADS, 5 * w + N_HEADS, 6 * w + N_HEADS]
    qa, ka, va, fa, qb, kb, vb = (w_in[:, cuts[i]:cuts[i + 1]] for i in range(7))
    w_perm = jnp.concatenate([qa, ka, va, qb, kb, vb, fa, jnp.zeros((d, LANES - N_HEADS), F32)], axis=1).astype(BF16)
    p = _norm_matmul(x, 0, d, norm, w_perm)
    ccol, crow = _log_decay(p, 6 * N_PAIRS, forget_bias, b, s)
    out_a = _fox_attention(p, ccol, crow, gq, gk, b, s)
    out_b = _sb_attention(p, 3 * N_PAIRS, b, s)
    return _proj_residual(out_a, out_b, w_out, x)


def _odd_layer(x, b, s, norm, w_in, gq_moba, gk_moba, q_a_norm, w_q_b, kv_a_norm, w_kv_b, gq_mla, gk_mla,
               w_out, rel_bias, bias_tiles, rope_tables):
    d = x.shape[1]
    n_main = 3 * HEAD_DIM * N_HEADS + MLA_Q_LORA + MLA_KV_LORA
    w_perm = jnp.concatenate([
        w_in[:, :n_main], jnp.zeros((d, MLA_NOPE), F32), w_in[:, n_main:],
        jnp.zeros((d, LANES - MLA_QK), F32)], axis=1).astype(BF16)
    p = _norm_matmul(x, 0, d, norm, w_perm)
    out_c = _moba_attention(p, rel_bias, bias_tiles[0], bias_tiles[1], gq_moba, gk_moba, b, s)
    lat0 = 3 * HEAD_DIM * N_HEADS
    w_q = jnp.pad(w_q_b.reshape(MLA_Q_LORA, N_HEADS, MLA_QK), ((0, 0), (0, 0), (0, LANES - MLA_QK)))
    qd = _norm_matmul(p, lat0 // MLA_Q_LORA, MLA_Q_LORA, q_a_norm, w_q.reshape(MLA_Q_LORA, N_HEADS * LANES).astype(BF16))
    kvd = _norm_matmul(p, (lat0 + MLA_Q_LORA) // MLA_KV_LORA, MLA_KV_LORA, kv_a_norm, w_kv_b.astype(BF16))
    out_d = _mla_attention(qd, kvd, p, (n_main // LANES), rope_tables, gq_mla, gk_mla, b, s)
    return _proj_residual(out_c, out_d, w_out, x)


def kernel(x, ffn_norm, ffn_w_gate_up, ffn_w_down, rel_bias, ev_norm, ev_w_in, ev_forget_bias, ev_fox_q_norm, ev_fox_k_norm, ev_w_out, od_norm, od_w_in, od_moba_q_norm, od_moba_k_norm, od_mla_q_a_norm, od_mla_w_q_b, od_mla_kv_a_norm, od_mla_w_kv_b, od_mla_q_norm, od_mla_k_norm, od_w_out):
    b, s, d = x.shape
    depth = ffn_norm.shape[0]
    bias_tiles = _bias_tiles(rel_bias)
    rope_tables = _rope_tables(s)
    x = x.reshape(b * s, d)
    for layer in range(depth):
        i = layer // 2
        if layer % 2 == 0:
            x = _even_layer(x, b, s, ev_norm[i], ev_w_in[i], ev_forget_bias[i], ev_fox_q_norm[i],
                            ev_fox_k_norm[i], ev_w_out[i])
        else:
            x = _odd_layer(x, b, s, od_norm[i], od_w_in[i], od_moba_q_norm[i], od_moba_k_norm[i],
                           od_mla_q_a_norm[i], od_mla_w_q_b[i], od_mla_kv_a_norm[i], od_mla_w_kv_b[i],
                           od_mla_q_norm[i], od_mla_k_norm[i], od_w_out[i], rel_bias, bias_tiles, rope_tables)
        x = _ffn(x, ffn_norm[layer], ffn_w_gate_up[layer], ffn_w_down[layer])
    return x.reshape(b, s, d)
```

```python
import functools
import math

import jax
import jax.numpy as jnp
from jax import lax
from jax.experimental import pallas as pl
from jax.experimental.pallas import tpu as pltpu

F32 = jnp.float32
BF16 = jnp.bfloat16

HEAD_DIM = 64
N_HEADS = 8
N_PAIRS = N_HEADS // 2
LANES = 128
MOBA_BLOCK = 256
MOBA_TOPK = 3
MLA_Q_LORA = 256
MLA_KV_LORA = 128
MLA_NOPE = 64
MLA_ROPE = 32
MLA_QK = MLA_NOPE + MLA_ROPE
ROPE_BASE = 10000.0
REL_BUCKETS = 32
REL_MAX_EXACT = 16
REL_MAX_DISTANCE = 128
RMS_EPS = 1e-6
TQ = 256
TK = 256
PREP_ROWS = 512
DEAD_LOG = -110.0
VMEM_LIMIT = 56 * 1024 * 1024
HIGHEST = lax.Precision.HIGHEST


def _cparams(*sem):
    return pltpu.CompilerParams(dimension_semantics=sem, vmem_limit_bytes=VMEM_LIMIT)


def _dot(a, b, **kw):
    return jnp.dot(a, b, preferred_element_type=F32, **kw)


def _log_sigmoid_parts(z):
    sp = jnp.log1p(jnp.exp(-jnp.abs(z)))
    return jnp.minimum(z, 0.0) - sp, -jnp.maximum(z, 0.0) - sp


def _lane_iota():
    return lax.broadcasted_iota(jnp.int32, (1, LANES), 1)


def _row_iota():
    return lax.broadcasted_iota(jnp.int32, (LANES, 1), 0)


def _pair_rms(x, gain, first):
    sq = x * x
    s0 = jnp.sum(jnp.where(first, sq, 0.0), axis=1, keepdims=True)
    s1 = jnp.sum(jnp.where(first, 0.0, sq), axis=1, keepdims=True)
    ms = jnp.where(first, s0, s1) * (1.0 / HEAD_DIM)
    return x * lax.rsqrt(ms + RMS_EPS) * gain


def _split3(c):
    hi = c.astype(BF16).astype(F32)
    mid = (c - hi).astype(BF16).astype(F32)
    return hi, mid, c - hi - mid


def _norm_matmul_kernel(x_ref, g_ref, w_ref, o_ref, *, n_chunk):
    x = x_ref[...]
    ms = jnp.mean(x * x, axis=-1, keepdims=True)
    h = (x * lax.rsqrt(ms + RMS_EPS) * g_ref[...]).astype(BF16)
    n = o_ref.shape[1]
    for c0 in range(0, n, n_chunk):
        c1 = min(n, c0 + n_chunk)
        o_ref[:, c0:c1] = _dot(h, w_ref[:, c0:c1])


def _norm_matmul(xw, col_blk, k, gain, w, tm=256, n_chunk=512):
    m = xw.shape[0]
    n = w.shape[1]
    return pl.pallas_call(
        functools.partial(_norm_matmul_kernel, n_chunk=n_chunk),
        grid=(m // tm,),
        in_specs=[
            pl.BlockSpec((tm, k), lambda i: (i, col_blk)),
            pl.BlockSpec((1, k), lambda i: (0, 0)),
            pl.BlockSpec((k, n), lambda i: (0, 0)),
        ],
        out_specs=pl.BlockSpec((tm, n), lambda i: (i, 0)),
        out_shape=jax.ShapeDtypeStruct((m, n), F32),
        compiler_params=_cparams("parallel"),
        name="norm_matmul",
    )(xw, gain.reshape(1, k), w)


def _proj_res_kernel(a_ref, b_ref, wa_ref, wb_ref, r_ref, o_ref):
    o_ref[...] = r_ref[...] + (_dot(a_ref[...], wa_ref[...]) + _dot(b_ref[...], wb_ref[...]))


def _proj_residual(a, b, w_out, res, tm=512):
    m, d = res.shape
    ka = a.shape[1]
    w = w_out.astype(BF16)
    return pl.pallas_call(
        _proj_res_kernel,
        grid=(m // tm,),
        in_specs=[
            pl.BlockSpec((tm, ka), lambda i: (i, 0)),
            pl.BlockSpec((tm, ka), lambda i: (i, 0)),
            pl.BlockSpec((ka, d), lambda i: (0, 0)),
            pl.BlockSpec((ka, d), lambda i: (1, 0)),
            pl.BlockSpec((tm, d), lambda i: (i, 0)),
        ],
        out_specs=pl.BlockSpec((tm, d), lambda i: (i, 0)),
        out_shape=jax.ShapeDtypeStruct((m, d), F32),
        compiler_params=_cparams("parallel"),
        name="proj_residual",
    )(a, b, w, w, res)


def _ffn_kernel(x_ref, g_ref, wg_ref, wu_ref, wd_ref, o_ref, h_s, acc_s):
    c = pl.program_id(1)

    @pl.when(c == 0)
    def _():
        x = x_ref[...]
        ms = jnp.mean(x * x, axis=-1, keepdims=True)
        h_s[...] = (x * lax.rsqrt(ms + RMS_EPS) * g_ref[...]).astype(BF16)
        acc_s[...] = jnp.zeros_like(acc_s)

    h = h_s[...]
    g = _dot(h, wg_ref[...])
    u = _dot(h, wu_ref[...])
    act = (g * jax.nn.sigmoid(g) * u).astype(BF16)
    acc_s[...] += _dot(act, wd_ref[...])

    @pl.when(c == pl.num_programs(1) - 1)
    def _():
        o_ref[...] = x_ref[...] + acc_s[...]


def _ffn(x, gain, w_gate_up, w_down, tm=512, n_ff_chunks=2):
    m, d = x.shape
    d_ff = w_down.shape[0]
    tf = d_ff // n_ff_chunks
    assert tf * n_ff_chunks == d_ff and tf % LANES == 0
    wgu = w_gate_up.astype(BF16)
    wd = w_down.astype(BF16)
    return pl.pallas_call(
        _ffn_kernel,
        grid=(m // tm, n_ff_chunks),
        in_specs=[
            pl.BlockSpec((tm, d), lambda i, c: (i, 0)),
            pl.BlockSpec((1, d), lambda i, c: (0, 0)),
            pl.BlockSpec((d, tf), lambda i, c: (0, c)),
            pl.BlockSpec((d, tf), lambda i, c: (0, n_ff_chunks + c)),
            pl.BlockSpec((tf, d), lambda i, c: (c, 0)),
        ],
        out_specs=pl.BlockSpec((tm, d), lambda i, c: (i, 0)),
        out_shape=jax.ShapeDtypeStruct((m, d), F32),
        scratch_shapes=[pltpu.VMEM((tm, d), BF16), pltpu.VMEM((tm, d), F32)],
        compiler_params=_cparams("parallel", "arbitrary"),
        name="swiglu_ffn",
    )(x, gain.reshape(1, d), wgu, wgu, wd)


def _decay_kernel(f_ref, b_ref, c_ref, carry_s):
    @pl.when(pl.program_id(1) == 0)
    def _():
        carry_s[...] = jnp.zeros_like(carry_s)

    ts = f_ref.shape[0]
    logf, _ = _log_sigmoid_parts(f_ref[...] + b_ref[...])
    r = lax.broadcasted_iota(jnp.int32, (ts, ts), 0)
    c = lax.broadcasted_iota(jnp.int32, (ts, ts), 1)
    tri = (c <= r).astype(F32)
    cs = _dot(tri, logf, precision=HIGHEST) + carry_s[...]
    carry_s[...] = cs[ts - 1:ts, :]
    c_ref[...] = cs[:, 0:N_HEADS]


def _log_decay(p, col_blk, bias, b, s, ts=256):
    ns = s // ts
    bias128 = jnp.zeros((1, LANES), F32).at[0, :N_HEADS].set(bias)
    return pl.pallas_call(
        _decay_kernel,
        grid=(b, ns),
        in_specs=[
            pl.BlockSpec((ts, LANES), lambda bi, si: (bi * ns + si, col_blk)),
            pl.BlockSpec((1, LANES), lambda bi, si: (0, 0)),
        ],
        out_specs=pl.BlockSpec((ts, N_HEADS), lambda bi, si: (bi * ns + si, 0)),
        out_shape=jax.ShapeDtypeStruct((b * s, N_HEADS), F32),
        scratch_shapes=[pltpu.VMEM((1, LANES), F32)],
        compiler_params=_cparams("parallel", "arbitrary"),
        name="fox_log_decay",
    )(p, bias128)


def _attend(first_scores, first_chunk, scores_fn, vt_s, s_s, p_s):
    w = 2 * TQ
    s_s[0] = first_scores
    p_s[0] = jnp.zeros((TK, w), BF16)

    def values(c, pr, al, a0, a1):
        k0 = pl.multiple_of(c * TK, TK)
        a0 = al[:, 0:TQ] * a0 + _dot(vt_s[0, :, pl.ds(k0, TK)], pr[:, 0:TQ])
        a1 = al[:, TQ:w] * a1 + _dot(vt_s[1, :, pl.ds(k0, TK)], pr[:, TQ:w])
        return a0, a1

    def body(i, carry):
        m, al, a0, a1 = carry
        par = lax.rem(i, 2)
        c = first_chunk - i
        pr_prev = p_s[par]
        s = s_s[par]
        a0, a1 = values(jnp.minimum(c + 1, first_chunk), pr_prev, al, a0, a1)
        s_next = scores_fn(jnp.maximum(c - 1, 0))
        m_new = jnp.maximum(m, jnp.max(s, axis=0, keepdims=True))
        pr = jnp.exp(s - m_new).astype(BF16)
        s_s[1 - par] = s_next
        p_s[1 - par] = pr
        return m_new, jnp.exp(m - m_new), a0, a1

    init = (jnp.full((1, w), -jnp.inf, F32), jnp.ones((1, w), F32),
            jnp.zeros((LANES, TQ), F32), jnp.zeros((LANES, TQ), F32))
    n = first_chunk + 1
    _, al, a0, a1 = lax.fori_loop(0, n, body, init)
    a0, a1 = values(0, p_s[lax.rem(n, 2)], al, a0, a1)
    return (a0[0:HEAD_DIM, :] / a0[HEAD_DIM:HEAD_DIM + 1, :],
            a1[0:HEAD_DIM, :] / a1[HEAD_DIM:HEAD_DIM + 1, :])


def _score_scratch():
    return [pltpu.VMEM((2, TK, 2 * TQ), F32), pltpu.VMEM((2, TK, 2 * TQ), BF16)]


def _key_query_iotas(nq=TQ):
    k = lax.broadcasted_iota(jnp.int32, (TK, nq), 0)
    q = lax.broadcasted_iota(jnp.int32, (TK, nq), 1)
    return k, q


def _pair_causal(strict=False):
    kidx, qidx = _key_query_iotas(2 * TQ)
    qidx = jnp.where(qidx >= TQ, qidx - TQ, qidx)
    return kidx < qidx if strict else kidx <= qidx


def _values_with_ones(vt):
    return jnp.where(_row_iota() < HEAD_DIM, vt, 1.0)


def _prep_loop(n_rows, fn):
    def body(i, carry):
        fn(pl.multiple_of(i * PREP_ROWS, PREP_ROWS))
        return carry

    lax.fori_loop(0, n_rows // PREP_ROWS, body, 0)


def _pair_specs(nq, s, qcol, kcol, vcol):
    return [
        pl.BlockSpec((TQ, LANES), lambda b, hp, qi: (b * nq + qi, qcol + hp)),
        pl.BlockSpec((s, LANES), lambda b, hp, qi: (b, kcol + hp)),
        pl.BlockSpec((s, LANES), lambda b, hp, qi: (b, vcol + hp)),
    ]


def _store_pair(o_ref, out0, out1):
    o_ref[...] = jnp.concatenate([out0, out1], axis=0).T.astype(o_ref.dtype)


def _fox_kernel(q_ref, k_ref, v_ref, cq_ref, ck_ref, gq_ref, gk_ref, o_ref, ka_s, vt_s, s_s, p_s, *, seq):
    hp = pl.program_id(1)
    qi = pl.program_id(2)
    lane = _lane_iota()
    first = lane < HEAD_DIM
    head_lane = lax.broadcasted_iota(jnp.int32, (1, N_HEADS), 1)

    def augment(x, c, key_side):
        hi, mid, lo = _split3(c)
        sgn = -1.0 if key_side else 1.0
        c0 = HEAD_DIM + (3 if key_side else 0)
        one0 = HEAD_DIM + (0 if key_side else 3)
        out = jnp.where(first, x, 0.0)
        out = jnp.where(jnp.logical_and(lane >= one0, lane < one0 + 3), 1.0, out)
        for i, t in enumerate((hi, mid, lo)):
            out = jnp.where(lane == c0 + i, sgn * t, out)
        return out

    def head_col(cc, h):
        return jnp.sum(jnp.where(head_lane == h, cc, 0.0), axis=1, keepdims=True)

    @pl.when(qi == 0)
    def _():
        def prep(r0):
            rs = pl.ds(r0, PREP_ROWS)
            kn = _pair_rms(k_ref[rs, :], gk_ref[...], first)
            vt = v_ref[rs, :].T
            cc = ck_ref[rs, :]
            for j in range(2):
                kj = kn if j == 0 else pltpu.roll(kn, HEAD_DIM, 1)
                ka_s[j, rs, :] = augment(kj, head_col(cc, 2 * hp + j), True).astype(BF16)
                vj = vt if j == 0 else pltpu.roll(vt, HEAD_DIM, 0)
                vt_s[j, :, rs] = _values_with_ones(vj).astype(BF16)

        _prep_loop(seq, prep)

    qn = _pair_rms(q_ref[...], gq_ref[...], first) * (HEAD_DIM ** -0.5)
    cq = cq_ref[...]
    qt = []
    for j in range(2):
        qj = qn if j == 0 else pltpu.roll(qn, HEAD_DIM, 1)
        qt.append(augment(qj, head_col(cq, 2 * hp + j), False).T.astype(BF16))

    def scores(c):
        ks = pl.ds(pl.multiple_of(c * TK, TK), TK)
        return jnp.concatenate([_dot(ka_s[0, ks, :], qt[0]), _dot(ka_s[1, ks, :], qt[1])], axis=1)

    _store_pair(o_ref, *_attend(jnp.where(_pair_causal(), scores(qi), -jnp.inf), qi, scores, vt_s, s_s, p_s))


def _fox_attention(p, cdecay, gq, gk, b, s):
    nq = s // TQ
    gq2 = jnp.tile(gq, 2).reshape(1, LANES)
    gk2 = jnp.tile(gk, 2).reshape(1, LANES)
    return pl.pallas_call(
        functools.partial(_fox_kernel, seq=s),
        grid=(b, N_PAIRS, nq),
        in_specs=_pair_specs(nq, s, 0, N_PAIRS, 2 * N_PAIRS) + [
            pl.BlockSpec((TQ, N_HEADS), lambda b_, hp, qi: (b_ * nq + qi, 0)),
            pl.BlockSpec((s, N_HEADS), lambda b_, hp, qi: (b_, 0)),
            pl.BlockSpec((1, LANES), lambda b_, hp, qi: (0, 0)),
            pl.BlockSpec((1, LANES), lambda b_, hp, qi: (0, 0)),
        ],
        out_specs=pl.BlockSpec((TQ, LANES), lambda b_, hp, qi: (b_ * nq + qi, hp)),
        out_shape=jax.ShapeDtypeStruct((b * s, N_PAIRS * LANES), BF16),
        scratch_shapes=[pltpu.VMEM((2, s, LANES), BF16), pltpu.VMEM((2, LANES, s), BF16)] + _score_scratch(),
        compiler_params=_cparams("parallel", "parallel", "arbitrary"),
        name="fox_attention",
    )(p, p, p, cdecay, cdecay, gq2, gk2)


def _pair_queries_t(q, first):
    return jnp.concatenate([jnp.where(first, q, 0.0).T, jnp.where(first, 0.0, q).T], axis=1)


def _pair_outputs(acc):
    return acc[0:HEAD_DIM, 0:TQ], acc[HEAD_DIM:LANES, TQ:2 * TQ]


def _sb_kernel(q_ref, k_ref, v_ref, u_ref, o_ref, kb_s, vt_s, *, seq):
    qi = pl.program_id(2)
    first = _lane_iota() < HEAD_DIM

    @pl.when(qi == 0)
    def _():
        def prep(r0):
            rs = pl.ds(r0, PREP_ROWS)
            kb_s[rs, :] = k_ref[rs, :].astype(BF16)
            vt_s[:, rs] = v_ref[rs, :].T.astype(BF16)

        _prep_loop(seq, prep)

    qt = _pair_queries_t(q_ref[...] * (HEAD_DIM ** -0.5), first).astype(BF16)
    strict = _pair_causal(strict=True)

    def chunk(kj, rsum, acc, diag):
        k0 = pl.multiple_of(kj * TK, TK)
        z = _dot(kb_s[pl.ds(k0, TK), :], qt)
        log_b, log_1m = _log_sigmoid_parts(z)
        if diag:
            log_1m = jnp.where(strict, log_1m, 0.0)
        hi = log_1m.astype(BF16)
        lo = (log_1m - hi.astype(F32)).astype(BF16)
        between = _dot(u_ref[...], hi) + _dot(u_ref[...], lo)
        w = jnp.exp(log_b + rsum + between)
        if diag:
            w = jnp.where(strict, w, 0.0)
        acc = acc + _dot(vt_s[:, pl.ds(k0, TK)], w.astype(BF16))
        return rsum + jnp.sum(log_1m, axis=0, keepdims=True), acc

    rsum, acc = chunk(qi, jnp.zeros((1, 2 * TQ), F32), jnp.zeros((LANES, 2 * TQ), F32), True)

    def cond(c):
        return jnp.logical_and(c[0] >= 0, jnp.max(c[1]) > DEAD_LOG)

    def body(c):
        rs, ac = chunk(c[0], c[1], c[2], False)
        return c[0] - 1, rs, ac

    _, _, acc = lax.while_loop(cond, body, (qi - 1, rsum, acc))
    _store_pair(o_ref, *_pair_outputs(acc))


def _sb_attention(p, qcol, b, s):
    nq = s // TQ
    r = lax.broadcasted_iota(jnp.int32, (TK, TK), 0)
    c = lax.broadcasted_iota(jnp.int32, (TK, TK), 1)
    later = (c > r).astype(BF16)
    return pl.pallas_call(
        functools.partial(_sb_kernel, seq=s),
        grid=(b, N_PAIRS, nq),
        in_specs=_pair_specs(nq, s, qcol, qcol + N_PAIRS, qcol + 2 * N_PAIRS) + [
            pl.BlockSpec((TK, TK), lambda b_, hp, qi: (0, 0)),
        ],
        out_specs=pl.BlockSpec((TQ, LANES), lambda b_, hp, qi: (b_ * nq + qi, hp)),
        out_shape=jax.ShapeDtypeStruct((b * s, N_PAIRS * LANES), BF16),
        scratch_shapes=[pltpu.VMEM((s, LANES), BF16), pltpu.VMEM((LANES, s), BF16)],
        compiler_params=_cparams("parallel", "parallel", "arbitrary"),
        name="stick_breaking_attention",
    )(p, p, p, later)


def _t5_bucket(rel):
    n = jnp.maximum(rel, 0)
    nf = jnp.maximum(n, 1).astype(F32)
    large = REL_MAX_EXACT + (jnp.log(nf / REL_MAX_EXACT) / math.log(REL_MAX_DISTANCE / REL_MAX_EXACT)
                             * (REL_BUCKETS - REL_MAX_EXACT)).astype(jnp.int32)
    large = jnp.minimum(large, REL_BUCKETS - 1)
    return jnp.where(n < REL_MAX_EXACT, n, large)


def _bias_tile_kernel(rb_ref, own_ref, prev_ref):
    hp = pl.program_id(0)
    kidx, qidx = _key_query_iotas(MOBA_BLOCK)
    for off, ref in ((0, own_ref), (MOBA_BLOCK, prev_ref)):
        bucket = _t5_bucket(qidx - kidx + off)
        for j in range(2):
            out = jnp.zeros((MOBA_BLOCK, MOBA_BLOCK), F32)
            for bk in range(REL_BUCKETS):
                out = jnp.where(bucket == bk, rb_ref[2 * hp + j, bk], out)
            ref[0, :, j * MOBA_BLOCK:(j + 1) * MOBA_BLOCK] = out


def _bias_tiles(rel_bias):
    shape = jax.ShapeDtypeStruct((N_PAIRS, MOBA_BLOCK, 2 * MOBA_BLOCK), F32)
    spec = pl.BlockSpec((1, MOBA_BLOCK, 2 * MOBA_BLOCK), lambda h: (h, 0, 0))
    return pl.pallas_call(
        _bias_tile_kernel,
        grid=(N_PAIRS,),
        in_specs=[pl.BlockSpec(memory_space=pltpu.SMEM)],
        out_specs=[spec, spec],
        out_shape=[shape, shape],
        compiler_params=_cparams("arbitrary"),
        name="moba_bias_tiles",
    )(rel_bias)


def _moba_kernel(rb_ref, q_ref, k_ref, v_ref, bown_ref, bprev_ref, gq_ref, gk_ref, o_ref,
                 kn_s, vt_s, km_s, sel_s, s_s, p_s, *, seq):
    hp = pl.program_id(1)
    cur = pl.program_id(2)
    nb = seq // MOBA_BLOCK
    first = _lane_iota() < HEAD_DIM

    @pl.when(cur == 0)
    def _():
        def prep(n, carry):
            rs = pl.ds(pl.multiple_of(n * MOBA_BLOCK, MOBA_BLOCK), MOBA_BLOCK)
            kn = _pair_rms(k_ref[rs, :], gk_ref[...], first)
            kn_s[rs, :] = kn.astype(BF16)
            km_s[pl.ds(n, 1), :] = jnp.mean(kn, axis=0, keepdims=True)
            vt = v_ref[rs, :].T
            vt_s[0, :, rs] = _values_with_ones(vt).astype(BF16)
            vt_s[1, :, rs] = _values_with_ones(pltpu.roll(vt, HEAD_DIM, 0)).astype(BF16)
            return carry

        lax.fori_loop(0, nb, prep, 0)

    qf = _pair_queries_t(_pair_rms(q_ref[...], gq_ref[...], first) * (HEAD_DIM ** -0.5), first)
    qt = qf.astype(BF16)
    second = lax.broadcasted_iota(jnp.int32, (1, 2 * TQ), 1) >= TQ

    blk = lax.broadcasted_iota(jnp.int32, (nb, 1), 0)
    past = blk < cur
    gate = jnp.where(past, _dot(km_s[...], qf, precision=HIGHEST), -jnp.inf)
    beaten = jnp.zeros((nb, 2 * TQ), F32)
    for n2 in range(nb):
        g2 = gate[n2:n2 + 1, :]
        wins = jnp.logical_or(g2 > gate, jnp.logical_and(g2 == gate, n2 < blk))
        beaten = beaten + jnp.where(wins, 1.0, 0.0)
    sel_s[...] = jnp.where(jnp.logical_and(beaten < min(MOBA_TOPK, nb), past), 1.0, 0.0)

    far_bias = jnp.where(second, rb_ref[2 * hp + 1, REL_BUCKETS - 1], rb_ref[2 * hp, REL_BUCKETS - 1])

    def raw_scores(n):
        return _dot(kn_s[pl.ds(pl.multiple_of(n * MOBA_BLOCK, MOBA_BLOCK), MOBA_BLOCK), :], qt)

    def past_scores(n):
        s = raw_scores(n) + jnp.where(n == cur - 1, bprev_ref[0], far_bias)
        return jnp.where(sel_s[pl.ds(n, 1), :] > 0.5, s, -jnp.inf)

    own = jnp.where(_pair_causal(), raw_scores(cur) + bown_ref[0], -jnp.inf)
    _store_pair(o_ref, *_attend(own, cur, past_scores, vt_s, s_s, p_s))


def _moba_attention(p, rel_bias, bown, bprev, gq, gk, b, s):
    assert TQ == MOBA_BLOCK and TK == MOBA_BLOCK and s % MOBA_BLOCK == 0
    nb = s // MOBA_BLOCK
    gq2 = jnp.tile(gq, 2).reshape(1, LANES)
    gk2 = jnp.tile(gk, 2).reshape(1, LANES)
    bias_spec = pl.BlockSpec((1, MOBA_BLOCK, 2 * MOBA_BLOCK), lambda b_, hp, qi: (hp, 0, 0))
    return pl.pallas_call(
        functools.partial(_moba_kernel, seq=s),
        grid=(b, N_PAIRS, nb),
        in_specs=[pl.BlockSpec(memory_space=pltpu.SMEM)] + _pair_specs(nb, s, 0, N_PAIRS, 2 * N_PAIRS) + [
            bias_spec, bias_spec,
            pl.BlockSpec((1, LANES), lambda b_, hp, qi: (0, 0)),
            pl.BlockSpec((1, LANES), lambda b_, hp, qi: (0, 0)),
        ],
        out_specs=pl.BlockSpec((TQ, LANES), lambda b_, hp, qi: (b_ * nb + qi, hp)),
        out_shape=jax.ShapeDtypeStruct((b * s, N_PAIRS * LANES), BF16),
        scratch_shapes=[pltpu.VMEM((s, LANES), BF16), pltpu.VMEM((2, LANES, s), BF16),
                        pltpu.VMEM((nb, LANES), F32), pltpu.VMEM((nb, 2 * TQ), F32)] + _score_scratch(),
        compiler_params=_cparams("parallel", "parallel", "arbitrary"),
        name="moba_attention",
    )(rel_bias, p, p, p, bown, bprev, gq2, gk2)


def _rope(x, cos, sin_lo, sin_hi):
    return x * cos + pltpu.roll(x, LANES - MLA_ROPE // 2, 1) * sin_lo + pltpu.roll(x, MLA_ROPE // 2, 1) * sin_hi


def _mla_norm_rope(x, gain, cos, sin_lo, sin_hi):
    ms = jnp.sum(x * x, axis=1, keepdims=True) * (1.0 / MLA_QK)
    return _rope(x * lax.rsqrt(ms + RMS_EPS) * gain, cos, sin_lo, sin_hi)


def _mla_kernel(q_ref, kv_ref, kr_ref, ck_ref, slk_ref, shk_ref, cq_ref, slq_ref, shq_ref, gq_ref, gk_ref,
                o_ref, kn_s, vt_s, s_s, p_s, *, seq):
    qi = pl.program_id(2)
    first = _lane_iota() < HEAD_DIM

    @pl.when(qi == 0)
    def _():
        def prep(r0):
            rs = pl.ds(r0, PREP_ROWS)
            for j in range(2):
                kv = kv_ref[rs, j * LANES:(j + 1) * LANES]
                k = jnp.where(first, kv, kr_ref[rs, :])
                kn_s[j, rs, :] = _mla_norm_rope(k, gk_ref[...], ck_ref[rs, :], slk_ref[rs, :], shk_ref[rs, :]).astype(BF16)
                vt_s[j, :, rs] = _values_with_ones(pltpu.roll(kv.T, HEAD_DIM, 0)).astype(BF16)

        _prep_loop(seq, prep)

    qt = [_mla_norm_rope(q_ref[:, j * LANES:(j + 1) * LANES], gq_ref[...],
                         cq_ref[...], slq_ref[...], shq_ref[...]).T.astype(BF16) for j in range(2)]
    scale = MLA_QK ** -0.5

    def scores(c):
        ks = pl.ds(pl.multiple_of(c * TK, TK), TK)
        return jnp.concatenate([_dot(kn_s[0, ks, :], qt[0]), _dot(kn_s[1, ks, :], qt[1])], axis=1) * scale

    _store_pair(o_ref, *_attend(jnp.where(_pair_causal(), scores(qi), -jnp.inf), qi, scores, vt_s, s_s, p_s))


def _mla_attention(qd, kvd, p, kr_col, tables, gq, gk, b, s):
    nq = s // TQ
    cos, sin_lo, sin_hi = tables
    gq128 = jnp.zeros((1, LANES), F32).at[0, :MLA_QK].set(gq)
    gk128 = jnp.zeros((1, LANES), F32).at[0, :MLA_QK].set(gk)
    full = pl.BlockSpec((s, LANES), lambda b_, hp, qi: (0, 0))
    rowsq = pl.BlockSpec((TQ, LANES), lambda b_, hp, qi: (qi, 0))
    gain = pl.BlockSpec((1, LANES), lambda b_, hp, qi: (0, 0))
    return pl.pallas_call(
        functools.partial(_mla_kernel, seq=s),
        grid=(b, N_PAIRS, nq),
        in_specs=[
            pl.BlockSpec((TQ, 2 * LANES), lambda b_, hp, qi: (b_ * nq + qi, hp)),
            pl.BlockSpec((s, 2 * LANES), lambda b_, hp, qi: (b_, hp)),
            pl.BlockSpec((s, LANES), lambda b_, hp, qi: (b_, kr_col)),
            full, full, full, rowsq, rowsq, rowsq, gain, gain,
        ],
        out_specs=pl.BlockSpec((TQ, LANES), lambda b_, hp, qi: (b_ * nq + qi, hp)),
        out_shape=jax.ShapeDtypeStruct((b * s, N_PAIRS * LANES), BF16),
        scratch_shapes=[pltpu.VMEM((2, s, LANES), BF16), pltpu.VMEM((2, LANES, s), BF16)] + _score_scratch(),
        compiler_params=_cparams("parallel", "parallel", "arbitrary"),
        name="mla_attention",
    )(qd, kvd, p, cos, sin_lo, sin_hi, cos, sin_lo, sin_hi, gq128, gk128)


def _rope_tables(s):
    half = MLA_ROPE // 2
    inv_freq = ROPE_BASE ** (-jnp.arange(half, dtype=F32) / half)
    ang = jnp.arange(s, dtype=F32)[:, None] * inv_freq[None, :]
    cos, sin = jnp.cos(ang), jnp.sin(ang)
    z = lambda w: jnp.zeros((s, w), F32)
    cos_t = jnp.concatenate([jnp.ones((s, MLA_NOPE), F32), cos, cos, z(LANES - MLA_QK)], axis=1)
    sin_lo = jnp.concatenate([z(MLA_NOPE), -sin, z(LANES - MLA_NOPE - half)], axis=1)
    sin_hi = jnp.concatenate([z(MLA_NOPE + half), sin, z(LANES - MLA_QK)], axis=1)
    return cos_t, sin_lo, sin_hi


def _even_layer(x, b, s, norm, w_in, forget_bias, gq, gk, w_out):
    d = x.shape[1]
    w = HEAD_DIM * N_HEADS
    cuts = [0, w, 2 * w, 3 * w, 3 * w + N_HEADS, 4 * w + N_HEADS, 5 * w + N_HEADS, 6 * w + N_HEADS]
    qa, ka, va, fa, qb, kb, vb = (w_in[:, cuts[i]:cuts[i + 1]] for i in range(7))
    w_perm = jnp.concatenate([qa, ka, va, qb, kb, vb, fa, jnp.zeros((d, LANES - N_HEADS), F32)], axis=1).astype(BF16)
    p = _norm_matmul(x, 0, d, norm, w_perm)
    cdecay = _log_decay(p, 6 * N_PAIRS, forget_bias, b, s)
    out_a = _fox_attention(p, cdecay, gq, gk, b, s)
    out_b = _sb_attention(p, 3 * N_PAIRS, b, s)
    return _proj_residual(out_a, out_b, w_out, x)


def _odd_layer(x, b, s, norm, w_in, gq_moba, gk_moba, q_a_norm, w_q_b, kv_a_norm, w_kv_b, gq_mla, gk_mla,
               w_out, rel_bias, bias_tiles, rope_tables):
    d = x.shape[1]
    n_main = 3 * HEAD_DIM * N_HEADS + MLA_Q_LORA + MLA_KV_LORA
    w_perm = jnp.concatenate([
        w_in[:, :n_main], jnp.zeros((d, MLA_NOPE), F32), w_in[:, n_main:],
        jnp.zeros((d, LANES - MLA_QK), F32)], axis=1).astype(BF16)
    p = _norm_matmul(x, 0, d, norm, w_perm)
    out_c = _moba_attention(p, rel_bias, bias_tiles[0], bias_tiles[1], gq_moba, gk_moba, b, s)
    lat0 = 3 * HEAD_DIM * N_HEADS
    w_q = jnp.pad(w_q_b.reshape(MLA_Q_LORA, N_HEADS, MLA_QK), ((0, 0), (0, 0), (0, LANES - MLA_QK)))
    qd = _norm_matmul(p, lat0 // MLA_Q_LORA, MLA_Q_LORA, q_a_norm, w_q.reshape(MLA_Q_LORA, N_HEADS * LANES).astype(BF16))
    kvd = _norm_matmul(p, (lat0 + MLA_Q_LORA) // MLA_KV_LORA, MLA_KV_LORA, kv_a_norm, w_kv_b.astype(BF16))
    out_d = _mla_attention(qd, kvd, p, (n_main // LANES), rope_tables, gq_mla, gk_mla, b, s)
    return _proj_residual(out_c, out_d, w_out, x)


def kernel(x, ffn_norm, ffn_w_gate_up, ffn_w_down, rel_bias, ev_norm, ev_w_in, ev_forget_bias, ev_fox_q_norm, ev_fox_k_norm, ev_w_out, od_norm, od_w_in, od_moba_q_norm, od_moba_k_norm, od_mla_q_a_norm, od_mla_w_q_b, od_mla_kv_a_norm, od_mla_w_kv_b, od_mla_q_norm, od_mla_k_norm, od_w_out):
    b, s, d = x.shape
    depth = ffn_norm.shape[0]
    bias_tiles = _bias_tiles(rel_bias)
    rope_tables = _rope_tables(s)
    x = x.reshape(b * s, d)
    for layer in range(depth):
        i = layer // 2
        if layer % 2 == 0:
            x = _even_layer(x, b, s, ev_norm[i], ev_w_in[i], ev_forget_bias[i], ev_fox_q_norm[i],
                            ev_fox_k_norm[i], ev_w_out[i])
        else:
            x = _odd_layer(x, b, s, od_norm[i], od_w_in[i], od_moba_q_norm[i], od_moba_k_norm[i],
                           od_mla_q_a_norm[i], od_mla_w_q_b[i], od_mla_kv_a_norm[i], od_mla_w_kv_b[i],
                           od_mla_q_norm[i], od_mla_k_norm[i], od_w_out[i], rel_bias, bias_tiles, rope_tables)
        x = _ffn(x, ffn_norm[layer], ffn_w_gate_up[layer], ffn_w_down[layer])
    return x.reshape(b, s, d)
```

```python
import functools
import math

import jax
import jax.numpy as jnp
from jax import lax
from jax.experimental import pallas as pl
from jax.experimental.pallas import tpu as pltpu

F32 = jnp.float32
BF16 = jnp.bfloat16

HEAD_DIM = 64
N_HEADS = 8
N_PAIRS = N_HEADS // 2
LANES = 128
MOBA_BLOCK = 256
MOBA_TOPK = 3
MLA_Q_LORA = 256
MLA_KV_LORA = 128
MLA_NOPE = 64
MLA_ROPE = 32
MLA_QK = MLA_NOPE + MLA_ROPE
ROPE_BASE = 10000.0
REL_BUCKETS = 32
REL_MAX_EXACT = 16
REL_MAX_DISTANCE = 128
RMS_EPS = 1e-6
TQ = 256
TK = 256
KG = 2 * TK
VT_ROWS = 80
PREP_ROWS = 512
DEAD_LOG = -110.0
VMEM_LIMIT = 56 * 1024 * 1024
HIGHEST = lax.Precision.HIGHEST


def _cparams(*sem):
    return pltpu.CompilerParams(dimension_semantics=sem, vmem_limit_bytes=VMEM_LIMIT)


def _dot(a, b, **kw):
    return jnp.dot(a, b, preferred_element_type=F32, **kw)


def _log_sigmoid_parts(z):
    sp = jnp.log1p(jnp.exp(-jnp.abs(z)))
    return jnp.minimum(z, 0.0) - sp, -jnp.maximum(z, 0.0) - sp


def _lane_iota():
    return lax.broadcasted_iota(jnp.int32, (1, LANES), 1)


def _row_iota():
    return lax.broadcasted_iota(jnp.int32, (LANES, 1), 0)


def _pair_rms(x, gain, first):
    sq = x * x
    s0 = jnp.sum(jnp.where(first, sq, 0.0), axis=1, keepdims=True)
    s1 = jnp.sum(jnp.where(first, 0.0, sq), axis=1, keepdims=True)
    ms = jnp.where(first, s0, s1) * (1.0 / HEAD_DIM)
    return x * lax.rsqrt(ms + RMS_EPS) * gain


def _split3(c):
    hi = c.astype(BF16).astype(F32)
    mid = (c - hi).astype(BF16).astype(F32)
    return hi, mid, c - hi - mid


def _norm_matmul_kernel(x_ref, g_ref, w_ref, o_ref, *, n_chunk):
    x = x_ref[...]
    ms = jnp.mean(x * x, axis=-1, keepdims=True)
    h = (x * lax.rsqrt(ms + RMS_EPS) * g_ref[...]).astype(BF16)
    n = o_ref.shape[1]
    for c0 in range(0, n, n_chunk):
        c1 = min(n, c0 + n_chunk)
        o_ref[:, c0:c1] = _dot(h, w_ref[:, c0:c1])


def _norm_matmul(xw, col_blk, k, gain, w, tm=256, n_chunk=512):
    m = xw.shape[0]
    n = w.shape[1]
    return pl.pallas_call(
        functools.partial(_norm_matmul_kernel, n_chunk=n_chunk),
        grid=(m // tm,),
        in_specs=[
            pl.BlockSpec((tm, k), lambda i: (i, col_blk)),
            pl.BlockSpec((1, k), lambda i: (0, 0)),
            pl.BlockSpec((k, n), lambda i: (0, 0)),
        ],
        out_specs=pl.BlockSpec((tm, n), lambda i: (i, 0)),
        out_shape=jax.ShapeDtypeStruct((m, n), F32),
        compiler_params=_cparams("parallel"),
        name="norm_matmul",
    )(xw, gain.reshape(1, k), w)


def _proj_res_kernel(a_ref, b_ref, wa_ref, wb_ref, r_ref, o_ref):
    o_ref[...] = r_ref[...] + (_dot(a_ref[...], wa_ref[...]) + _dot(b_ref[...], wb_ref[...]))


def _proj_residual(a, b, w_out, res, tm=512):
    m, d = res.shape
    ka = a.shape[1]
    w = w_out.astype(BF16)
    return pl.pallas_call(
        _proj_res_kernel,
        grid=(m // tm,),
        in_specs=[
            pl.BlockSpec((tm, ka), lambda i: (i, 0)),
            pl.BlockSpec((tm, ka), lambda i: (i, 0)),
            pl.BlockSpec((ka, d), lambda i: (0, 0)),
            pl.BlockSpec((ka, d), lambda i: (1, 0)),
            pl.BlockSpec((tm, d), lambda i: (i, 0)),
        ],
        out_specs=pl.BlockSpec((tm, d), lambda i: (i, 0)),
        out_shape=jax.ShapeDtypeStruct((m, d), F32),
        compiler_params=_cparams("parallel"),
        name="proj_residual",
    )(a, b, w, w, res)


def _ffn_kernel(x_ref, g_ref, wg_ref, wu_ref, wd_ref, o_ref, h_s, acc_s):
    c = pl.program_id(1)

    @pl.when(c == 0)
    def _():
        x = x_ref[...]
        ms = jnp.mean(x * x, axis=-1, keepdims=True)
        h_s[...] = (x * lax.rsqrt(ms + RMS_EPS) * g_ref[...]).astype(BF16)
        acc_s[...] = jnp.zeros_like(acc_s)

    h = h_s[...]
    g = _dot(h, wg_ref[...])
    u = _dot(h, wu_ref[...])
    act = (g * jax.nn.sigmoid(g) * u).astype(BF16)
    acc_s[...] += _dot(act, wd_ref[...])

    @pl.when(c == pl.num_programs(1) - 1)
    def _():
        o_ref[...] = x_ref[...] + acc_s[...]


def _ffn(x, gain, w_gate_up, w_down, tm=512, n_ff_chunks=2):
    m, d = x.shape
    d_ff = w_down.shape[0]
    tf = d_ff // n_ff_chunks
    assert tf * n_ff_chunks == d_ff and tf % LANES == 0
    wgu = w_gate_up.astype(BF16)
    wd = w_down.astype(BF16)
    return pl.pallas_call(
        _ffn_kernel,
        grid=(m // tm, n_ff_chunks),
        in_specs=[
            pl.BlockSpec((tm, d), lambda i, c: (i, 0)),
            pl.BlockSpec((1, d), lambda i, c: (0, 0)),
            pl.BlockSpec((d, tf), lambda i, c: (0, c)),
            pl.BlockSpec((d, tf), lambda i, c: (0, n_ff_chunks + c)),
            pl.BlockSpec((tf, d), lambda i, c: (c, 0)),
        ],
        out_specs=pl.BlockSpec((tm, d), lambda i, c: (i, 0)),
        out_shape=jax.ShapeDtypeStruct((m, d), F32),
        scratch_shapes=[pltpu.VMEM((tm, d), BF16), pltpu.VMEM((tm, d), F32)],
        compiler_params=_cparams("parallel", "arbitrary"),
        name="swiglu_ffn",
    )(x, gain.reshape(1, d), wgu, wgu, wd)


def _decay_kernel(f_ref, b_ref, c_ref, carry_s):
    @pl.when(pl.program_id(1) == 0)
    def _():
        carry_s[...] = jnp.zeros_like(carry_s)

    ts = f_ref.shape[0]
    logf, _ = _log_sigmoid_parts(f_ref[...] + b_ref[...])
    r = lax.broadcasted_iota(jnp.int32, (ts, ts), 0)
    c = lax.broadcasted_iota(jnp.int32, (ts, ts), 1)
    tri = (c <= r).astype(F32)
    cs = _dot(tri, logf, precision=HIGHEST) + carry_s[...]
    carry_s[...] = cs[ts - 1:ts, :]
    c_ref[...] = cs[:, 0:N_HEADS]


def _log_decay(p, col_blk, bias, b, s, ts=256):
    ns = s // ts
    bias128 = jnp.zeros((1, LANES), F32).at[0, :N_HEADS].set(bias)
    return pl.pallas_call(
        _decay_kernel,
        grid=(b, ns),
        in_specs=[
            pl.BlockSpec((ts, LANES), lambda bi, si: (bi * ns + si, col_blk)),
            pl.BlockSpec((1, LANES), lambda bi, si: (0, 0)),
        ],
        out_specs=pl.BlockSpec((ts, N_HEADS), lambda bi, si: (bi * ns + si, 0)),
        out_shape=jax.ShapeDtypeStruct((b * s, N_HEADS), F32),
        scratch_shapes=[pltpu.VMEM((1, LANES), F32)],
        compiler_params=_cparams("parallel", "arbitrary"),
        name="fox_log_decay",
    )(p, bias128)


def _attend(first_scores, first_group, scores_fn, vt_s, s_s, p_s, alive_fn=None):
    w = 2 * TQ
    s_s[0] = first_scores
    p_s[0] = jnp.zeros((KG, w), BF16)

    def values(g, pr, al, a0, a1):
        k0 = pl.multiple_of(g * KG, KG)
        a0 = al[:, 0:TQ] * a0 + _dot(vt_s[0, :, pl.ds(k0, KG)], pr[:, 0:TQ])
        a1 = al[:, TQ:w] * a1 + _dot(vt_s[1, :, pl.ds(k0, KG)], pr[:, TQ:w])
        return a0, a1

    def cond(carry):
        return jnp.logical_and(carry[0] <= first_group, carry[1] > 0)

    def body(carry):
        i, _, m, al, a0, a1 = carry
        par = lax.rem(i, 2)
        g = first_group - i
        pr_prev = p_s[par]
        s = s_s[par]
        a0, a1 = values(jnp.minimum(g + 1, first_group), pr_prev, al, a0, a1)
        s_next = scores_fn(jnp.maximum(g - 1, 0))
        m_new = jnp.maximum(m, jnp.max(s, axis=0, keepdims=True))
        pr = jnp.exp(s - m_new).astype(BF16)
        s_s[1 - par] = s_next
        p_s[1 - par] = pr
        alive = jnp.int32(1) if alive_fn is None else alive_fn(jnp.maximum(g - 1, 0), m_new).astype(jnp.int32)
        return i + 1, alive, m_new, jnp.exp(m - m_new), a0, a1

    init = (jnp.int32(0), jnp.int32(1), jnp.full((1, w), -jnp.inf, F32), jnp.ones((1, w), F32),
            jnp.zeros((VT_ROWS, TQ), F32), jnp.zeros((VT_ROWS, TQ), F32))
    n, _, _, al, a0, a1 = lax.while_loop(cond, body, init)
    a0, a1 = values(first_group - (n - 1), p_s[lax.rem(n, 2)], al, a0, a1)
    return (a0[0:HEAD_DIM, :] / a0[HEAD_DIM:HEAD_DIM + 1, :],
            a1[0:HEAD_DIM, :] / a1[HEAD_DIM:HEAD_DIM + 1, :])


def _score_scratch():
    return [pltpu.VMEM((2, KG, 2 * TQ), F32), pltpu.VMEM((2, KG, 2 * TQ), BF16)]


def _key_query_iotas(nk, nq):
    k = lax.broadcasted_iota(jnp.int32, (nk, nq), 0)
    q = lax.broadcasted_iota(jnp.int32, (nk, nq), 1)
    return k, q


def _pair_causal(nk, shift=0, strict=False):
    kidx, qidx = _key_query_iotas(nk, 2 * TQ)
    qidx = jnp.where(qidx >= TQ, qidx - TQ, qidx) + shift
    return kidx < qidx if strict else kidx <= qidx


def _values_with_ones(vt):
    return jnp.where(_row_iota() < HEAD_DIM, vt, 1.0)[0:VT_ROWS, :]


def _prep_loop(n_rows, fn):
    def body(i, carry):
        fn(pl.multiple_of(i * PREP_ROWS, PREP_ROWS))
        return carry

    lax.fori_loop(0, n_rows // PREP_ROWS, body, 0)


def _pair_specs(nq, s, qcol, kcol, vcol):
    return [
        pl.BlockSpec((TQ, LANES), lambda b, hp, qi: (b * nq + qi, qcol + hp)),
        pl.BlockSpec((s, LANES), lambda b, hp, qi: (b, kcol + hp)),
        pl.BlockSpec((s, LANES), lambda b, hp, qi: (b, vcol + hp)),
    ]


def _store_pair(o_ref, out0, out1):
    o_ref[...] = jnp.concatenate([out0, out1], axis=0).T.astype(o_ref.dtype)


def _fox_kernel(q_ref, k_ref, v_ref, cq_ref, ck_ref, gq_ref, gk_ref, o_ref, ka_s, vt_s, kmax_s, s_s, p_s, *, seq):
    hp = pl.program_id(1)
    qi = pl.program_id(2)
    lane = _lane_iota()
    first = lane < HEAD_DIM
    head_lane = lax.broadcasted_iota(jnp.int32, (1, N_HEADS), 1)
    second = lax.broadcasted_iota(jnp.int32, (1, 2 * TQ), 1) >= TQ

    def augment(x, c, key_side):
        hi, mid, lo = _split3(c)
        sgn = -1.0 if key_side else 1.0
        c0 = HEAD_DIM + (3 if key_side else 0)
        one0 = HEAD_DIM + (0 if key_side else 3)
        out = jnp.where(first, x, 0.0)
        out = jnp.where(jnp.logical_and(lane >= one0, lane < one0 + 3), 1.0, out)
        for i, t in enumerate((hi, mid, lo)):
            out = jnp.where(lane == c0 + i, sgn * t, out)
        return out

    def head_col(cc, h):
        return jnp.sum(jnp.where(head_lane == h, cc, 0.0), axis=1, keepdims=True)

    @pl.when(qi == 0)
    def _():
        kmax_s[...] = jnp.zeros_like(kmax_s)

        def prep(r0):
            rs = pl.ds(r0, PREP_ROWS)
            kn = _pair_rms(k_ref[rs, :], gk_ref[...], first)
            vt = v_ref[rs, :].T
            cc = ck_ref[rs, :]
            for j in range(2):
                kj = kn if j == 0 else pltpu.roll(kn, HEAD_DIM, 1)
                ka_s[j, rs, :] = augment(kj, head_col(cc, 2 * hp + j), True).astype(BF16)
                vj = vt if j == 0 else pltpu.roll(vt, HEAD_DIM, 0)
                vt_s[j, :, rs] = _values_with_ones(vj).astype(BF16)
                sq = jnp.sum(jnp.where(first, kj * kj, 0.0), axis=1, keepdims=True)
                kmax_s[j] = jnp.maximum(kmax_s[j], jnp.max(sq, axis=0, keepdims=True))

        _prep_loop(seq, prep)

    qn = _pair_rms(q_ref[...], gq_ref[...], first) * (HEAD_DIM ** -0.5)
    cq = cq_ref[...]
    qt, qk_bound, c_query = [], [], []
    for j in range(2):
        qj = qn if j == 0 else pltpu.roll(qn, HEAD_DIM, 1)
        qa = augment(qj, head_col(cq, 2 * hp + j), False).T
        qt.append(qa.astype(BF16))
        qsq = jnp.sum(qa[0:HEAD_DIM, :] * qa[0:HEAD_DIM, :], axis=0, keepdims=True)
        qk_bound.append(1.02 * jnp.sqrt(qsq * kmax_s[j][:, 0:1]))
        c_query.append(jnp.sum(qa[HEAD_DIM:HEAD_DIM + 3, :], axis=0, keepdims=True))
    score_bound = jnp.concatenate(qk_bound, axis=1) + jnp.concatenate(c_query, axis=1)

    def scores(g):
        ks = pl.ds(pl.multiple_of(g * KG, KG), KG)
        return jnp.concatenate([_dot(ka_s[0, ks, :], qt[0]), _dot(ka_s[1, ks, :], qt[1])], axis=1)

    def alive(g, m):
        row = ck_ref[pl.ds((g + 1) * KG - 1, 1), :]
        c0 = jnp.sum(jnp.where(head_lane == 2 * hp, row, 0.0), axis=1, keepdims=True)
        c1 = jnp.sum(jnp.where(head_lane == 2 * hp + 1, row, 0.0), axis=1, keepdims=True)
        return jnp.max(score_bound - jnp.where(second, c1, c0) - m) > DEAD_LOG

    g0 = qi // (KG // TQ)
    diag = _pair_causal(KG, (qi - g0 * (KG // TQ)) * TQ)
    _store_pair(o_ref, *_attend(jnp.where(diag, scores(g0), -jnp.inf), g0, scores, vt_s, s_s, p_s, alive))


def _fox_attention(p, cdecay, gq, gk, b, s):
    nq = s // TQ
    gq2 = jnp.tile(gq, 2).reshape(1, LANES)
    gk2 = jnp.tile(gk, 2).reshape(1, LANES)
    return pl.pallas_call(
        functools.partial(_fox_kernel, seq=s),
        grid=(b, N_PAIRS, nq),
        in_specs=_pair_specs(nq, s, 0, N_PAIRS, 2 * N_PAIRS) + [
            pl.BlockSpec((TQ, N_HEADS), lambda b_, hp, qi: (b_ * nq + qi, 0)),
            pl.BlockSpec((s, N_HEADS), lambda b_, hp, qi: (b_, 0)),
            pl.BlockSpec((1, LANES), lambda b_, hp, qi: (0, 0)),
            pl.BlockSpec((1, LANES), lambda b_, hp, qi: (0, 0)),
        ],
        out_specs=pl.BlockSpec((TQ, LANES), lambda b_, hp, qi: (b_ * nq + qi, hp)),
        out_shape=jax.ShapeDtypeStruct((b * s, N_PAIRS * LANES), BF16),
        scratch_shapes=[pltpu.VMEM((2, s, LANES), BF16), pltpu.VMEM((2, VT_ROWS, s), BF16),
                        pltpu.VMEM((2, 1, LANES), F32)] + _score_scratch(),
        compiler_params=_cparams("parallel", "parallel", "arbitrary"),
        name="fox_attention",
    )(p, p, p, cdecay, cdecay, gq2, gk2)


def _pair_queries_t(q, first):
    return jnp.concatenate([jnp.where(first, q, 0.0).T, jnp.where(first, 0.0, q).T], axis=1)


def _pair_outputs(acc):
    return acc[0:HEAD_DIM, 0:TQ], acc[HEAD_DIM:LANES, TQ:2 * TQ]


def _sb_kernel(q_ref, k_ref, v_ref, u_ref, o_ref, kb_s, vt_s, *, seq):
    qi = pl.program_id(2)
    first = _lane_iota() < HEAD_DIM

    @pl.when(qi == 0)
    def _():
        def prep(r0):
            rs = pl.ds(r0, PREP_ROWS)
            kb_s[rs, :] = k_ref[rs, :].astype(BF16)
            vt_s[:, rs] = v_ref[rs, :].T.astype(BF16)

        _prep_loop(seq, prep)

    qt = _pair_queries_t(q_ref[...] * (HEAD_DIM ** -0.5), first).astype(BF16)
    strict = _pair_causal(TK, strict=True)

    def chunk(kj, rsum, acc, diag):
        k0 = pl.multiple_of(kj * TK, TK)
        z = _dot(kb_s[pl.ds(k0, TK), :], qt)
        log_b, log_1m = _log_sigmoid_parts(z)
        if diag:
            log_1m = jnp.where(strict, log_1m, 0.0)
        hi = log_1m.astype(BF16)
        lo = (log_1m - hi.astype(F32)).astype(BF16)
        between = _dot(u_ref[...], hi) + _dot(u_ref[...], lo)
        w = jnp.exp(log_b + rsum + between)
        if diag:
            w = jnp.where(strict, w, 0.0)
        acc = acc + _dot(vt_s[:, pl.ds(k0, TK)], w.astype(BF16))
        return rsum + jnp.sum(log_1m, axis=0, keepdims=True), acc

    rsum, acc = chunk(qi, jnp.zeros((1, 2 * TQ), F32), jnp.zeros((LANES, 2 * TQ), F32), True)

    def cond(c):
        return jnp.logical_and(c[0] >= 0, c[1] > 0)

    def alive(rs):
        return (jnp.max(rs) > DEAD_LOG).astype(jnp.int32)

    def body(c):
        rs, ac = chunk(c[0], c[2], c[3], False)
        return c[0] - 1, alive(rs), rs, ac

    _, _, _, acc = lax.while_loop(cond, body, (qi - 1, alive(rsum), rsum, acc))
    _store_pair(o_ref, *_pair_outputs(acc))


def _sb_attention(p, qcol, b, s):
    nq = s // TQ
    r = lax.broadcasted_iota(jnp.int32, (TK, TK), 0)
    c = lax.broadcasted_iota(jnp.int32, (TK, TK), 1)
    later = (c > r).astype(BF16)
    return pl.pallas_call(
        functools.partial(_sb_kernel, seq=s),
        grid=(b, N_PAIRS, nq),
        in_specs=_pair_specs(nq, s, qcol, qcol + N_PAIRS, qcol + 2 * N_PAIRS) + [
            pl.BlockSpec((TK, TK), lambda b_, hp, qi: (0, 0)),
        ],
        out_specs=pl.BlockSpec((TQ, LANES), lambda b_, hp, qi: (b_ * nq + qi, hp)),
        out_shape=jax.ShapeDtypeStruct((b * s, N_PAIRS * LANES), BF16),
        scratch_shapes=[pltpu.VMEM((s, LANES), BF16), pltpu.VMEM((LANES, s), BF16)],
        compiler_params=_cparams("parallel", "parallel", "arbitrary"),
        name="stick_breaking_attention",
    )(p, p, p, later)


def _t5_bucket(rel):
    n = jnp.maximum(rel, 0)
    nf = jnp.maximum(n, 1).astype(F32)
    large = REL_MAX_EXACT + (jnp.log(nf / REL_MAX_EXACT) / math.log(REL_MAX_DISTANCE / REL_MAX_EXACT)
                             * (REL_BUCKETS - REL_MAX_EXACT)).astype(jnp.int32)
    large = jnp.minimum(large, REL_BUCKETS - 1)
    return jnp.where(n < REL_MAX_EXACT, n, large)


def _bias_tile_kernel(rb_ref, own_ref, prev_ref):
    hp = pl.program_id(0)
    kidx, qidx = _key_query_iotas(MOBA_BLOCK, MOBA_BLOCK)
    for off, ref in ((0, own_ref), (MOBA_BLOCK, prev_ref)):
        bucket = _t5_bucket(qidx - kidx + off)
        for j in range(2):
            out = jnp.zeros((MOBA_BLOCK, MOBA_BLOCK), F32)
            for bk in range(REL_BUCKETS):
                out = jnp.where(bucket == bk, rb_ref[2 * hp + j, bk], out)
            ref[0, :, j * MOBA_BLOCK:(j + 1) * MOBA_BLOCK] = out


def _bias_tiles(rel_bias):
    shape = jax.ShapeDtypeStruct((N_PAIRS, MOBA_BLOCK, 2 * MOBA_BLOCK), F32)
    spec = pl.BlockSpec((1, MOBA_BLOCK, 2 * MOBA_BLOCK), lambda h: (h, 0, 0))
    return pl.pallas_call(
        _bias_tile_kernel,
        grid=(N_PAIRS,),
        in_specs=[pl.BlockSpec(memory_space=pltpu.SMEM)],
        out_specs=[spec, spec],
        out_shape=[shape, shape],
        compiler_params=_cparams("arbitrary"),
        name="moba_bias_tiles",
    )(rel_bias)


def _moba_kernel(rb_ref, q_ref, k_ref, v_ref, bown_ref, bprev_ref, gq_ref, gk_ref, o_ref,
                 kn_s, vt_s, km_s, sel_s, s_s, p_s, *, seq):
    hp = pl.program_id(1)
    cur = pl.program_id(2)
    nb = seq // MOBA_BLOCK
    first = _lane_iota() < HEAD_DIM

    @pl.when(cur == 0)
    def _():
        def prep(n, carry):
            rs = pl.ds(pl.multiple_of(n * MOBA_BLOCK, MOBA_BLOCK), MOBA_BLOCK)
            kn = _pair_rms(k_ref[rs, :], gk_ref[...], first)
            kn_s[rs, :] = kn.astype(BF16)
            km_s[pl.ds(n, 1), :] = jnp.mean(kn, axis=0, keepdims=True)
            vt = v_ref[rs, :].T
            vt_s[0, :, rs] = _values_with_ones(vt).astype(BF16)
            vt_s[1, :, rs] = _values_with_ones(pltpu.roll(vt, HEAD_DIM, 0)).astype(BF16)
            return carry

        lax.fori_loop(0, nb, prep, 0)

    qf = _pair_queries_t(_pair_rms(q_ref[...], gq_ref[...], first) * (HEAD_DIM ** -0.5), first)
    qt = qf.astype(BF16)
    second = lax.broadcasted_iota(jnp.int32, (1, 2 * TQ), 1) >= TQ

    blk = lax.broadcasted_iota(jnp.int32, (nb, 1), 0)
    past = blk < cur
    gate = jnp.where(past, _dot(km_s[...], qf, precision=HIGHEST), -jnp.inf)
    beaten = jnp.zeros((nb, 2 * TQ), F32)
    for n2 in range(nb):
        g2 = gate[n2:n2 + 1, :]
        wins = jnp.logical_or(g2 > gate, jnp.logical_and(g2 == gate, n2 < blk))
        beaten = beaten + jnp.where(wins, 1.0, 0.0)
    sel_s[...] = jnp.where(jnp.logical_and(beaten < min(MOBA_TOPK, nb), past), 1.0, 0.0)

    far_bias = jnp.where(second, rb_ref[2 * hp + 1, REL_BUCKETS - 1], rb_ref[2 * hp, REL_BUCKETS - 1])

    per_group = KG // MOBA_BLOCK

    def group_scores(g, own_possible):
        raw = _dot(kn_s[pl.ds(pl.multiple_of(g * KG, KG), KG), :], qt)
        pieces = []
        for i in range(per_group):
            n = g * per_group + i
            s = raw[i * MOBA_BLOCK:(i + 1) * MOBA_BLOCK, :]
            piece = jnp.where(sel_s[pl.ds(n, 1), :] > 0.5,
                              s + jnp.where(n == cur - 1, bprev_ref[0], far_bias), -jnp.inf)
            if own_possible:
                own = jnp.where(_pair_causal(MOBA_BLOCK), s + bown_ref[0], -jnp.inf)
                piece = jnp.where(n == cur, own, piece)
            pieces.append(piece)
        return jnp.concatenate(pieces, axis=0)

    g0 = cur // per_group
    _store_pair(o_ref, *_attend(group_scores(g0, True), g0, lambda g: group_scores(g, False), vt_s, s_s, p_s))


def _moba_attention(p, rel_bias, bown, bprev, gq, gk, b, s):
    assert TQ == MOBA_BLOCK and KG % MOBA_BLOCK == 0 and s % KG == 0
    nb = s // MOBA_BLOCK
    gq2 = jnp.tile(gq, 2).reshape(1, LANES)
    gk2 = jnp.tile(gk, 2).reshape(1, LANES)
    bias_spec = pl.BlockSpec((1, MOBA_BLOCK, 2 * MOBA_BLOCK), lambda b_, hp, qi: (hp, 0, 0))
    return pl.pallas_call(
        functools.partial(_moba_kernel, seq=s),
        grid=(b, N_PAIRS, nb),
        in_specs=[pl.BlockSpec(memory_space=pltpu.SMEM)] + _pair_specs(nb, s, 0, N_PAIRS, 2 * N_PAIRS) + [
            bias_spec, bias_spec,
            pl.BlockSpec((1, LANES), lambda b_, hp, qi: (0, 0)),
            pl.BlockSpec((1, LANES), lambda b_, hp, qi: (0, 0)),
        ],
        out_specs=pl.BlockSpec((TQ, LANES), lambda b_, hp, qi: (b_ * nb + qi, hp)),
        out_shape=jax.ShapeDtypeStruct((b * s, N_PAIRS * LANES), BF16),
        scratch_shapes=[pltpu.VMEM((s, LANES), BF16), pltpu.VMEM((2, VT_ROWS, s), BF16),
                        pltpu.VMEM((nb, LANES), F32), pltpu.VMEM((nb, 2 * TQ), F32)] + _score_scratch(),
        compiler_params=_cparams("parallel", "parallel", "arbitrary"),
        name="moba_attention",
    )(rel_bias, p, p, p, bown, bprev, gq2, gk2)


def _rope(x, cos, sin_lo, sin_hi):
    return x * cos + pltpu.roll(x, LANES - MLA_ROPE // 2, 1) * sin_lo + pltpu.roll(x, MLA_ROPE // 2, 1) * sin_hi


def _mla_norm_rope(x, gain, cos, sin_lo, sin_hi):
    ms = jnp.sum(x * x, axis=1, keepdims=True) * (1.0 / MLA_QK)
    return _rope(x * lax.rsqrt(ms + RMS_EPS) * gain, cos, sin_lo, sin_hi)


def _mla_kernel(q_ref, kv_ref, kr_ref, ck_ref, slk_ref, shk_ref, cq_ref, slq_ref, shq_ref, gq_ref, gk_ref,
                o_ref, kn_s, vt_s, s_s, p_s, *, seq):
    qi = pl.program_id(2)
    first = _lane_iota() < HEAD_DIM

    @pl.when(qi == 0)
    def _():
        def prep(r0):
            rs = pl.ds(r0, PREP_ROWS)
            for j in range(2):
                kv = kv_ref[rs, j * LANES:(j + 1) * LANES]
                k = jnp.where(first, kv, kr_ref[rs, :])
                kn_s[j, rs, :] = _mla_norm_rope(k, gk_ref[...], ck_ref[rs, :], slk_ref[rs, :], shk_ref[rs, :]).astype(BF16)
                vt_s[j, :, rs] = _values_with_ones(pltpu.roll(kv.T, HEAD_DIM, 0)).astype(BF16)

        _prep_loop(seq, prep)

    qt = [_mla_norm_rope(q_ref[:, j * LANES:(j + 1) * LANES], gq_ref[...],
                         cq_ref[...], slq_ref[...], shq_ref[...]).T.astype(BF16) for j in range(2)]
    scale = MLA_QK ** -0.5

    def scores(g):
        ks = pl.ds(pl.multiple_of(g * KG, KG), KG)
        return jnp.concatenate([_dot(kn_s[0, ks, :], qt[0]), _dot(kn_s[1, ks, :], qt[1])], axis=1) * scale

    g0 = qi // (KG // TQ)
    diag = _pair_causal(KG, (qi - g0 * (KG // TQ)) * TQ)
    _store_pair(o_ref, *_attend(jnp.where(diag, scores(g0), -jnp.inf), g0, scores, vt_s, s_s, p_s))


def _mla_attention(qd, kvd, p, kr_col, tables, gq, gk, b, s):
    nq = s // TQ
    cos, sin_lo, sin_hi = tables
    gq128 = jnp.zeros((1, LANES), F32).at[0, :MLA_QK].set(gq)
    gk128 = jnp.zeros((1, LANES), F32).at[0, :MLA_QK].set(gk)
    full = pl.BlockSpec((s, LANES), lambda b_, hp, qi: (0, 0))
    rowsq = pl.BlockSpec((TQ, LANES), lambda b_, hp, qi: (qi, 0))
    gain = pl.BlockSpec((1, LANES), lambda b_, hp, qi: (0, 0))
    return pl.pallas_call(
        functools.partial(_mla_kernel, seq=s),
        grid=(b, N_PAIRS, nq),
        in_specs=[
            pl.BlockSpec((TQ, 2 * LANES), lambda b_, hp, qi: (b_ * nq + qi, hp)),
            pl.BlockSpec((s, 2 * LANES), lambda b_, hp, qi: (b_, hp)),
            pl.BlockSpec((s, LANES), lambda b_, hp, qi: (b_, kr_col)),
            full, full, full, rowsq, rowsq, rowsq, gain, gain,
        ],
        out_specs=pl.BlockSpec((TQ, LANES), lambda b_, hp, qi: (b_ * nq + qi, hp)),
        out_shape=jax.ShapeDtypeStruct((b * s, N_PAIRS * LANES), BF16),
        scratch_shapes=[pltpu.VMEM((2, s, LANES), BF16), pltpu.VMEM((2, VT_ROWS, s), BF16)] + _score_scratch(),
        compiler_params=_cparams("parallel", "parallel", "arbitrary"),
        name="mla_attention",
    )(qd, kvd, p, cos, sin_lo, sin_hi, cos, sin_lo, sin_hi, gq128, gk128)


def _rope_tables(s):
    half = MLA_ROPE // 2
    inv_freq = ROPE_BASE ** (-jnp.arange(half, dtype=F32) / half)
    ang = jnp.arange(s, dtype=F32)[:, None] * inv_freq[None, :]
    cos, sin = jnp.cos(ang), jnp.sin(ang)
    z = lambda w: jnp.zeros((s, w), F32)
    cos_t = jnp.concatenate([jnp.ones((s, MLA_NOPE), F32), cos, cos, z(LANES - MLA_QK)], axis=1)
    sin_lo = jnp.concatenate([z(MLA_NOPE), -sin, z(LANES - MLA_NOPE - half)], axis=1)
    sin_hi = jnp.concatenate([z(MLA_NOPE + half), sin, z(LANES - MLA_QK)], axis=1)
    return cos_t, sin_lo, sin_hi


def _even_layer(x, b, s, norm, w_in, forget_bias, gq, gk, w_out):
    d = x.shape[1]
    w = HEAD_DIM * N_HEADS
    cuts = [0, w, 2 * w, 3 * w, 3 * w + N_HEADS, 4 * w + N_HEADS, 5 * w + N_HEADS, 6 * w + N_HEADS]
    qa, ka, va, fa, qb, kb, vb = (w_in[:, cuts[i]:cuts[i + 1]] for i in range(7))
    w_perm = jnp.concatenate([qa, ka, va, qb, kb, vb, fa, jnp.zeros((d, LANES - N_HEADS), F32)], axis=1).astype(BF16)
    p = _norm_matmul(x, 0, d, norm, w_perm)
    cdecay = _log_decay(p, 6 * N_PAIRS, forget_bias, b, s)
    out_a = _fox_attention(p, cdecay, gq, gk, b, s)
    out_b = _sb_attention(p, 3 * N_PAIRS, b, s)
    return _proj_residual(out_a, out_b, w_out, x)


def _odd_layer(x, b, s, norm, w_in, gq_moba, gk_moba, q_a_norm, w_q_b, kv_a_norm, w_kv_b, gq_mla, gk_mla,
               w_out, rel_bias, bias_tiles, rope_tables):
    d = x.shape[1]
    n_main = 3 * HEAD_DIM * N_HEADS + MLA_Q_LORA + MLA_KV_LORA
    w_perm = jnp.concatenate([
        w_in[:, :n_main], jnp.zeros((d, MLA_NOPE), F32), w_in[:, n_main:],
        jnp.zeros((d, LANES - MLA_QK), F32)], axis=1).astype(BF16)
    p = _norm_matmul(x, 0, d, norm, w_perm)
    out_c = _moba_attention(p, rel_bias, bias_tiles[0], bias_tiles[1], gq_moba, gk_moba, b, s)
    lat0 = 3 * HEAD_DIM * N_HEADS
    w_q = jnp.pad(w_q_b.reshape(MLA_Q_LORA, N_HEADS, MLA_QK), ((0, 0), (0, 0), (0, LANES - MLA_QK)))
    qd = _norm_matmul(p, lat0 // MLA_Q_LORA, MLA_Q_LORA, q_a_norm, w_q.reshape(MLA_Q_LORA, N_HEADS * LANES).astype(BF16))
    kvd = _norm_matmul(p, (lat0 + MLA_Q_LORA) // MLA_KV_LORA, MLA_KV_LORA, kv_a_norm, w_kv_b.astype(BF16))
    out_d = _mla_attention(qd, kvd, p, (n_main // LANES), rope_tables, gq_mla, gk_mla, b, s)
    return _proj_residual(out_c, out_d, w_out, x)


def kernel(x, ffn_norm, ffn_w_gate_up, ffn_w_down, rel_bias, ev_norm, ev_w_in, ev_forget_bias, ev_fox_q_norm, ev_fox_k_norm, ev_w_out, od_norm, od_w_in, od_moba_q_norm, od_moba_k_norm, od_mla_q_a_norm, od_mla_w_q_b, od_mla_kv_a_norm, od_mla_w_kv_b, od_mla_q_norm, od_mla_k_norm, od_w_out):
    b, s, d = x.shape
    depth = ffn_norm.shape[0]
    bias_tiles = _bias_tiles(rel_bias)
    rope_tables = _rope_tables(s)
    x = x.reshape(b * s, d)
    for layer in range(depth):
        i = layer // 2
        if layer % 2 == 0:
            x = _even_layer(x, b, s, ev_norm[i], ev_w_in[i], ev_forget_bias[i], ev_fox_q_norm[i],
                            ev_fox_k_norm[i], ev_w_out[i])
        else:
            x = _odd_layer(x, b, s, od_norm[i], od_w_in[i], od_moba_q_norm[i], od_moba_k_norm[i],
                           od_mla_q_a_norm[i], od_mla_w_q_b[i], od_mla_kv_a_norm[i], od_mla_w_kv_b[i],
                           od_mla_q_norm[i], od_mla_k_norm[i], od_w_out[i], rel_bias, bias_tiles, rope_tables)
        x = _ffn(x, ffn_norm[layer], ffn_w_gate_up[layer], ffn_w_down[layer])
    return x.reshape(b, s, d)
```

```python
import functools
import math

import jax
import jax.numpy as jnp
from jax import lax
from jax.experimental import pallas as pl
from jax.experimental.pallas import tpu as pltpu

F32 = jnp.float32
BF16 = jnp.bfloat16

HEAD_DIM = 64
N_HEADS = 8
N_PAIRS = N_HEADS // 2
LANES = 128
MOBA_BLOCK = 256
MOBA_TOPK = 3
MLA_Q_LORA = 256
MLA_KV_LORA = 128
MLA_NOPE = 64
MLA_ROPE = 32
MLA_QK = MLA_NOPE + MLA_ROPE
ROPE_BASE = 10000.0
REL_BUCKETS = 32
REL_MAX_EXACT = 16
REL_MAX_DISTANCE = 128
RMS_EPS = 1e-6
TQ = 256
KG = 512
SB_TQ = 256
TK = 256
VT_ROWS = 80
PREP_ROWS = 512
DEAD_LOG = -110.0
VMEM_LIMIT = 56 * 1024 * 1024
HIGHEST = lax.Precision.HIGHEST


def _cparams(*sem):
    return pltpu.CompilerParams(dimension_semantics=sem, vmem_limit_bytes=VMEM_LIMIT)


def _dot(a, b, **kw):
    return jnp.dot(a, b, preferred_element_type=F32, **kw)


def _log_sigmoid(z):
    return jnp.minimum(z, 0.0) - jnp.log1p(jnp.exp(-jnp.abs(z)))


def _lane_iota():
    return lax.broadcasted_iota(jnp.int32, (1, LANES), 1)


def _row_iota():
    return lax.broadcasted_iota(jnp.int32, (LANES, 1), 0)


def _pair_rms(x, gain, first):
    sq = x * x
    s0 = jnp.sum(jnp.where(first, sq, 0.0), axis=1, keepdims=True)
    s1 = jnp.sum(jnp.where(first, 0.0, sq), axis=1, keepdims=True)
    ms = jnp.where(first, s0, s1) * (1.0 / HEAD_DIM)
    return x * lax.rsqrt(ms + RMS_EPS) * gain


def _split3(c):
    hi = c.astype(BF16).astype(F32)
    mid = (c - hi).astype(BF16).astype(F32)
    return hi, mid, c - hi - mid


def _norm_matmul_kernel(x_ref, g_ref, w_ref, o_ref, *, n_chunk):
    x = x_ref[...]
    ms = jnp.mean(x * x, axis=-1, keepdims=True)
    h = (x * lax.rsqrt(ms + RMS_EPS) * g_ref[...]).astype(BF16)
    n = o_ref.shape[1]
    for c0 in range(0, n, n_chunk):
        c1 = min(n, c0 + n_chunk)
        o_ref[:, c0:c1] = _dot(h, w_ref[:, c0:c1])


def _norm_matmul(xw, col_blk, k, gain, w, tm=256, n_chunk=512):
    m = xw.shape[0]
    n = w.shape[1]
    return pl.pallas_call(
        functools.partial(_norm_matmul_kernel, n_chunk=n_chunk),
        grid=(m // tm,),
        in_specs=[
            pl.BlockSpec((tm, k), lambda i: (i, col_blk)),
            pl.BlockSpec((1, k), lambda i: (0, 0)),
            pl.BlockSpec((k, n), lambda i: (0, 0)),
        ],
        out_specs=pl.BlockSpec((tm, n), lambda i: (i, 0)),
        out_shape=jax.ShapeDtypeStruct((m, n), F32),
        compiler_params=_cparams("parallel"),
        name="norm_matmul",
    )(xw, gain.reshape(1, k), w)


def _proj_res_kernel(a_ref, b_ref, wa_ref, wb_ref, r_ref, o_ref):
    o_ref[...] = r_ref[...] + (_dot(a_ref[...], wa_ref[...]) + _dot(b_ref[...], wb_ref[...]))


def _proj_residual(a, b, w_out, res, tm=512):
    m, d = res.shape
    ka = a.shape[1]
    w = w_out.astype(BF16)
    return pl.pallas_call(
        _proj_res_kernel,
        grid=(m // tm,),
        in_specs=[
            pl.BlockSpec((tm, ka), lambda i: (i, 0)),
            pl.BlockSpec((tm, ka), lambda i: (i, 0)),
            pl.BlockSpec((ka, d), lambda i: (0, 0)),
            pl.BlockSpec((ka, d), lambda i: (1, 0)),
            pl.BlockSpec((tm, d), lambda i: (i, 0)),
        ],
        out_specs=pl.BlockSpec((tm, d), lambda i: (i, 0)),
        out_shape=jax.ShapeDtypeStruct((m, d), F32),
        compiler_params=_cparams("parallel"),
        name="proj_residual",
    )(a, b, w, w, res)


def _ffn_kernel(x_ref, g_ref, wg_ref, wu_ref, wd_ref, o_ref, h_s, acc_s):
    c = pl.program_id(1)

    @pl.when(c == 0)
    def _():
        x = x_ref[...]
        ms = jnp.mean(x * x, axis=-1, keepdims=True)
        h_s[...] = (x * lax.rsqrt(ms + RMS_EPS) * g_ref[...]).astype(BF16)
        acc_s[...] = jnp.zeros_like(acc_s)

    h = h_s[...]
    g = _dot(h, wg_ref[...])
    u = _dot(h, wu_ref[...])
    act = (g * jax.nn.sigmoid(g) * u).astype(BF16)
    acc_s[...] += _dot(act, wd_ref[...])

    @pl.when(c == pl.num_programs(1) - 1)
    def _():
        o_ref[...] = x_ref[...] + acc_s[...]


def _ffn(x, gain, w_gate_up, w_down, tm=512, n_ff_chunks=2):
    m, d = x.shape
    d_ff = w_down.shape[0]
    tf = d_ff // n_ff_chunks
    assert tf * n_ff_chunks == d_ff and tf % LANES == 0
    wgu = w_gate_up.astype(BF16)
    wd = w_down.astype(BF16)
    return pl.pallas_call(
        _ffn_kernel,
        grid=(m // tm, n_ff_chunks),
        in_specs=[
            pl.BlockSpec((tm, d), lambda i, c: (i, 0)),
            pl.BlockSpec((1, d), lambda i, c: (0, 0)),
            pl.BlockSpec((d, tf), lambda i, c: (0, c)),
            pl.BlockSpec((d, tf), lambda i, c: (0, n_ff_chunks + c)),
            pl.BlockSpec((tf, d), lambda i, c: (c, 0)),
        ],
        out_specs=pl.BlockSpec((tm, d), lambda i, c: (i, 0)),
        out_shape=jax.ShapeDtypeStruct((m, d), F32),
        scratch_shapes=[pltpu.VMEM((tm, d), BF16), pltpu.VMEM((tm, d), F32)],
        compiler_params=_cparams("parallel", "arbitrary"),
        name="swiglu_ffn",
    )(x, gain.reshape(1, d), wgu, wgu, wd)


def _decay_kernel(f_ref, b_ref, c_ref, carry_s):
    @pl.when(pl.program_id(1) == 0)
    def _():
        carry_s[...] = jnp.zeros_like(carry_s)

    ts = f_ref.shape[0]
    logf = _log_sigmoid(f_ref[...] + b_ref[...])
    r = lax.broadcasted_iota(jnp.int32, (ts, ts), 0)
    c = lax.broadcasted_iota(jnp.int32, (ts, ts), 1)
    tri = (c <= r).astype(F32)
    cs = _dot(tri, logf, precision=HIGHEST) + carry_s[...]
    carry_s[...] = cs[ts - 1:ts, :]
    c_ref[...] = cs[:, 0:N_HEADS]


def _log_decay(p, col_blk, bias, b, s, ts=256):
    ns = s // ts
    bias128 = jnp.zeros((1, LANES), F32).at[0, :N_HEADS].set(bias)
    return pl.pallas_call(
        _decay_kernel,
        grid=(b, ns),
        in_specs=[
            pl.BlockSpec((ts, LANES), lambda bi, si: (bi * ns + si, col_blk)),
            pl.BlockSpec((1, LANES), lambda bi, si: (0, 0)),
        ],
        out_specs=pl.BlockSpec((ts, N_HEADS), lambda bi, si: (bi * ns + si, 0)),
        out_shape=jax.ShapeDtypeStruct((b * s, N_HEADS), F32),
        scratch_shapes=[pltpu.VMEM((1, LANES), F32)],
        compiler_params=_cparams("parallel", "arbitrary"),
        name="fox_log_decay",
    )(p, bias128)


def _attend(first_scores, first_group, scores_fn, vt_s, s_s, p_s, alive_fn=None):
    w = 2 * TQ
    s_s[0] = first_scores
    p_s[0] = jnp.zeros((KG, w), BF16)

    def values(g, pr, al, a0, a1):
        k0 = pl.multiple_of(g * KG, KG)
        a0 = al[:, 0:TQ] * a0 + _dot(vt_s[0, :, pl.ds(k0, KG)], pr[:, 0:TQ])
        a1 = al[:, TQ:w] * a1 + _dot(vt_s[1, :, pl.ds(k0, KG)], pr[:, TQ:w])
        return a0, a1

    def cond(carry):
        return jnp.logical_and(carry[0] <= first_group, carry[1] > 0)

    def body(carry):
        i, _, m, al, a0, a1 = carry
        par = lax.rem(i, 2)
        g = first_group - i
        pr_prev = p_s[par]
        s = s_s[par]
        a0, a1 = values(jnp.minimum(g + 1, first_group), pr_prev, al, a0, a1)
        s_next = scores_fn(jnp.maximum(g - 1, 0))
        m_new = jnp.maximum(m, jnp.max(s, axis=0, keepdims=True))
        pr = jnp.exp(s - m_new).astype(BF16)
        s_s[1 - par] = s_next
        p_s[1 - par] = pr
        alive = jnp.int32(1) if alive_fn is None else alive_fn(jnp.maximum(g - 1, 0), m_new).astype(jnp.int32)
        return i + 1, alive, m_new, jnp.exp(m - m_new), a0, a1

    init = (jnp.int32(0), jnp.int32(1), jnp.full((1, w), -jnp.inf, F32), jnp.ones((1, w), F32),
            jnp.zeros((VT_ROWS, TQ), F32), jnp.zeros((VT_ROWS, TQ), F32))
    n, _, _, al, a0, a1 = lax.while_loop(cond, body, init)
    a0, a1 = values(first_group - (n - 1), p_s[lax.rem(n, 2)], al, a0, a1)
    return (a0[0:HEAD_DIM, :] / a0[HEAD_DIM:HEAD_DIM + 1, :],
            a1[0:HEAD_DIM, :] / a1[HEAD_DIM:HEAD_DIM + 1, :])


def _score_scratch():
    return [pltpu.VMEM((2, KG, 2 * TQ), F32), pltpu.VMEM((2, KG, 2 * TQ), BF16)]


def _key_query_iotas(nk, nq):
    k = lax.broadcasted_iota(jnp.int32, (nk, nq), 0)
    q = lax.broadcasted_iota(jnp.int32, (nk, nq), 1)
    return k, q


def _pair_causal(nk, tq, shift=0, strict=False):
    kidx, qidx = _key_query_iotas(nk, 2 * tq)
    qidx = jnp.where(qidx >= tq, qidx - tq, qidx) + shift
    return kidx < qidx if strict else kidx <= qidx


def _values_with_ones(vt):
    return jnp.where(_row_iota() < HEAD_DIM, vt, 1.0)[0:VT_ROWS, :]


def _prep_loop(n_rows, fn):
    def body(i, carry):
        fn(pl.multiple_of(i * PREP_ROWS, PREP_ROWS))
        return carry

    lax.fori_loop(0, n_rows // PREP_ROWS, body, 0)


def _pair_specs(s, qcol, kcol, vcol):
    return [pl.BlockSpec((s, LANES), lambda b, hp, qi, col=col: (b, col + hp)) for col in (qcol, kcol, vcol)]


def _diag_group(qi):
    g0 = (qi * TQ + TQ - 1) // KG
    return g0, qi * TQ - g0 * KG


def _store_pair(o_ref, out0, out1):
    o_ref[...] = jnp.concatenate([out0, out1], axis=0).T.astype(o_ref.dtype)


def _fox_kernel(q_ref, k_ref, v_ref, c_ref, gq_ref, gk_ref, o_ref,
                qt_s, qb_s, ka_s, vt_s, kmax_s, s_s, p_s, *, seq):
    hp = pl.program_id(1)
    qi = pl.program_id(2)
    lane = _lane_iota()
    first = lane < HEAD_DIM
    head_lane = lax.broadcasted_iota(jnp.int32, (1, N_HEADS), 1)
    second = lax.broadcasted_iota(jnp.int32, (1, 2 * TQ), 1) >= TQ

    def augment(x, c, key_side):
        hi, mid, lo = _split3(c)
        sgn = -1.0 if key_side else 1.0
        c0 = HEAD_DIM + (3 if key_side else 0)
        one0 = HEAD_DIM + (0 if key_side else 3)
        out = jnp.where(first, x, 0.0)
        out = jnp.where(jnp.logical_and(lane >= one0, lane < one0 + 3), 1.0, out)
        for i, t in enumerate((hi, mid, lo)):
            out = jnp.where(lane == c0 + i, sgn * t, out)
        return out

    def head_col(cc, h):
        return jnp.sum(jnp.where(head_lane == h, cc, 0.0), axis=1, keepdims=True)

    @pl.when(qi == 0)
    def _():
        kmax_s[...] = jnp.zeros_like(kmax_s)

        def prep(r0):
            rs = pl.ds(r0, PREP_ROWS)
            kn = _pair_rms(k_ref[rs, :], gk_ref[...], first)
            qn = _pair_rms(q_ref[rs, :], gq_ref[...], first) * (HEAD_DIM ** -0.5)
            vt = v_ref[rs, :].T
            cc = c_ref[rs, :]
            for j in range(2):
                c = head_col(cc, 2 * hp + j)
                kj = kn if j == 0 else pltpu.roll(kn, HEAD_DIM, 1)
                ka_s[j, rs, :] = augment(kj, c, True).astype(BF16)
                vj = vt if j == 0 else pltpu.roll(vt, HEAD_DIM, 0)
                vt_s[j, :, rs] = _values_with_ones(vj).astype(BF16)
                sq = jnp.sum(jnp.where(first, kj * kj, 0.0), axis=1, keepdims=True)
                kmax_s[j] = jnp.maximum(kmax_s[j], jnp.max(sq, axis=0, keepdims=True))
                qa = augment(qn if j == 0 else pltpu.roll(qn, HEAD_DIM, 1), c, False).T
                qt_s[j, :, rs] = qa.astype(BF16)
                qb_s[j, 0:1, rs] = jnp.sqrt(jnp.sum(qa[0:HEAD_DIM, :] * qa[0:HEAD_DIM, :], axis=0, keepdims=True))
                qb_s[j, 1:2, rs] = jnp.sum(qa[HEAD_DIM:HEAD_DIM + 3, :], axis=0, keepdims=True)

        _prep_loop(seq, prep)

    qs = pl.ds(pl.multiple_of(qi * TQ, TQ), TQ)
    qt = [qt_s[0, :, qs], qt_s[1, :, qs]]
    score_bound = jnp.concatenate(
        [1.02 * qb_s[j, 0:1, qs] * jnp.sqrt(kmax_s[j][:, 0:1]) + qb_s[j, 1:2, qs] for j in range(2)], axis=1)

    def scores(g):
        ks = pl.ds(pl.multiple_of(g * KG, KG), KG)
        return jnp.concatenate([_dot(ka_s[0, ks, :], qt[0]), _dot(ka_s[1, ks, :], qt[1])], axis=1)

    def alive(g, m):
        row = c_ref[pl.ds((g + 1) * KG - 1, 1), :]
        c0 = jnp.sum(jnp.where(head_lane == 2 * hp, row, 0.0), axis=1, keepdims=True)
        c1 = jnp.sum(jnp.where(head_lane == 2 * hp + 1, row, 0.0), axis=1, keepdims=True)
        return jnp.max(score_bound - jnp.where(second, c1, c0) - m) > DEAD_LOG

    g0, shift = _diag_group(qi)
    diag = _pair_causal(KG, TQ, shift)
    _store_pair(o_ref, *_attend(jnp.where(diag, scores(g0), -jnp.inf), g0, scores, vt_s, s_s, p_s, alive))


def _fox_attention(p, cdecay, gq, gk, b, s):
    nq = s // TQ
    gq2 = jnp.tile(gq, 2).reshape(1, LANES)
    gk2 = jnp.tile(gk, 2).reshape(1, LANES)
    return pl.pallas_call(
        functools.partial(_fox_kernel, seq=s),
        grid=(b, N_PAIRS, nq),
        in_specs=_pair_specs(s, 0, N_PAIRS, 2 * N_PAIRS) + [
            pl.BlockSpec((s, N_HEADS), lambda b_, hp, qi: (b_, 0)),
            pl.BlockSpec((1, LANES), lambda b_, hp, qi: (0, 0)),
            pl.BlockSpec((1, LANES), lambda b_, hp, qi: (0, 0)),
        ],
        out_specs=pl.BlockSpec((TQ, LANES), lambda b_, hp, qi: (b_ * nq + qi, hp)),
        out_shape=jax.ShapeDtypeStruct((b * s, N_PAIRS * LANES), BF16),
        scratch_shapes=[pltpu.VMEM((2, LANES, s), BF16), pltpu.VMEM((2, 8, s), F32),
                        pltpu.VMEM((2, s, LANES), BF16), pltpu.VMEM((2, VT_ROWS, s), BF16),
                        pltpu.VMEM((2, 1, LANES), F32)] + _score_scratch(),
        compiler_params=_cparams("parallel", "parallel", "arbitrary"),
        name="fox_attention",
    )(p, p, p, cdecay, gq2, gk2)


def _sb_kernel(q_ref, k_ref, v_ref, u_ref, o_ref, qt_s, kb_s, vt_s, *, seq):
    qi = pl.program_id(2)
    first = _lane_iota() < HEAD_DIM
    w = 2 * SB_TQ

    @pl.when(qi == 0)
    def _():
        def prep(r0):
            rs = pl.ds(r0, PREP_ROWS)
            kb_s[rs, :] = k_ref[rs, :].astype(BF16)
            vt_s[:, rs] = v_ref[rs, :].T.astype(BF16)
            q = q_ref[rs, :] * (HEAD_DIM ** -0.5)
            qt_s[0, :, rs] = jnp.where(first, q, 0.0).T.astype(BF16)
            qt_s[1, :, rs] = jnp.where(first, 0.0, q).T.astype(BF16)

        _prep_loop(seq, prep)

    qs = pl.ds(pl.multiple_of(qi * SB_TQ, SB_TQ), SB_TQ)
    qt = jnp.concatenate([qt_s[0, :, qs], qt_s[1, :, qs]], axis=1)

    def chunk(k0, nk, drop, acc, strict):
        z = _dot(kb_s[pl.ds(k0, nk), :], qt)
        softplus = jnp.maximum(z, 0.0) + jnp.log(1.0 + jnp.exp(-jnp.abs(z)))
        log_b = z - softplus
        if strict is not None:
            softplus = jnp.where(strict, softplus, 0.0)
        hi = softplus.astype(BF16)
        lo = (softplus - hi.astype(F32)).astype(BF16)
        later = u_ref[0:nk, 0:nk]
        wgt = jnp.exp(log_b - (drop + (_dot(later, hi) + _dot(later, lo))))
        if strict is not None:
            wgt = jnp.where(strict, wgt, 0.0)
        acc = acc + _dot(vt_s[:, pl.ds(k0, nk)], wgt.astype(BF16))
        return drop + jnp.sum(softplus, axis=0, keepdims=True), acc

    k0 = pl.multiple_of(jnp.maximum(qi - 1, 0) * TK, TK)
    strict = _pair_causal(2 * TK, SB_TQ, qi * SB_TQ - k0, strict=True)
    rsum, acc = chunk(k0, 2 * TK, jnp.zeros((1, w), F32), jnp.zeros((LANES, w), F32), strict)

    def cond(c):
        return jnp.logical_and(c[0] >= 0, c[1] > 0)

    def alive(drop):
        return (jnp.min(drop) < -DEAD_LOG).astype(jnp.int32)

    def body(c):
        rs, ac = chunk(pl.multiple_of(c[0] * TK, TK), TK, c[2], c[3], None)
        return c[0] - 1, alive(rs), rs, ac

    _, _, _, acc = lax.while_loop(cond, body, (qi - 2, alive(rsum), rsum, acc))
    _store_pair(o_ref, acc[0:HEAD_DIM, 0:SB_TQ], acc[HEAD_DIM:LANES, SB_TQ:w])


def _sb_attention(p, qcol, b, s):
    assert SB_TQ == TK
    nq = s // SB_TQ
    r = lax.broadcasted_iota(jnp.int32, (2 * TK, 2 * TK), 0)
    c = lax.broadcasted_iota(jnp.int32, (2 * TK, 2 * TK), 1)
    later = (c > r).astype(BF16)
    return pl.pallas_call(
        functools.partial(_sb_kernel, seq=s),
        grid=(b, N_PAIRS, nq),
        in_specs=_pair_specs(s, qcol, qcol + N_PAIRS, qcol + 2 * N_PAIRS) + [
            pl.BlockSpec((2 * TK, 2 * TK), lambda b_, hp, qi: (0, 0)),
        ],
        out_specs=pl.BlockSpec((SB_TQ, LANES), lambda b_, hp, qi: (b_ * nq + qi, hp)),
        out_shape=jax.ShapeDtypeStruct((b * s, N_PAIRS * LANES), BF16),
        scratch_shapes=[pltpu.VMEM((2, LANES, s), BF16), pltpu.VMEM((s, LANES), BF16), pltpu.VMEM((LANES, s), BF16)],
        compiler_params=_cparams("parallel", "parallel", "arbitrary"),
        name="stick_breaking_attention",
    )(p, p, p, later)


def _t5_bucket(rel):
    n = jnp.maximum(rel, 0)
    nf = jnp.maximum(n, 1).astype(F32)
    large = REL_MAX_EXACT + (jnp.log(nf / REL_MAX_EXACT) / math.log(REL_MAX_DISTANCE / REL_MAX_EXACT)
                             * (REL_BUCKETS - REL_MAX_EXACT)).astype(jnp.int32)
    large = jnp.minimum(large, REL_BUCKETS - 1)
    return jnp.where(n < REL_MAX_EXACT, n, large)


def _bias_tile_kernel(rb_ref, own_ref, prev_ref):
    hp = pl.program_id(0)
    kidx, qidx = _key_query_iotas(MOBA_BLOCK, MOBA_BLOCK)
    for off, ref in ((0, own_ref), (MOBA_BLOCK, prev_ref)):
        bucket = _t5_bucket(qidx - kidx + off)
        for j in range(2):
            out = jnp.zeros((MOBA_BLOCK, MOBA_BLOCK), F32)
            for bk in range(REL_BUCKETS):
                out = jnp.where(bucket == bk, rb_ref[2 * hp + j, bk], out)
            ref[0, :, j * MOBA_BLOCK:(j + 1) * MOBA_BLOCK] = out - rb_ref[2 * hp + j, REL_BUCKETS - 1]


def _bias_tiles(rel_bias):
    shape = jax.ShapeDtypeStruct((N_PAIRS, MOBA_BLOCK, 2 * MOBA_BLOCK), F32)
    spec = pl.BlockSpec((1, MOBA_BLOCK, 2 * MOBA_BLOCK), lambda h: (h, 0, 0))
    return pl.pallas_call(
        _bias_tile_kernel,
        grid=(N_PAIRS,),
        in_specs=[pl.BlockSpec(memory_space=pltpu.SMEM)],
        out_specs=[spec, spec],
        out_shape=[shape, shape],
        compiler_params=_cparams("arbitrary"),
        name="moba_bias_tiles",
    )(rel_bias)


def _moba_kernel(q_ref, k_ref, v_ref, bown_ref, bprev_ref, gq_ref, gk_ref, o_ref,
                 qt_s, kn_s, vt_s, km_s, sel_s, s_s, p_s, *, seq):
    qi = pl.program_id(2)
    nb = seq // MOBA_BLOCK
    first = _lane_iota() < HEAD_DIM
    w = 2 * TQ
    per_step = TQ // MOBA_BLOCK
    per_group = KG // MOBA_BLOCK

    @pl.when(qi == 0)
    def _():
        def prep_keys(n, carry):
            rs = pl.ds(pl.multiple_of(n * MOBA_BLOCK, MOBA_BLOCK), MOBA_BLOCK)
            kn = _pair_rms(k_ref[rs, :], gk_ref[...], first)
            kn_s[rs, :] = kn.astype(BF16)
            km_s[pl.ds(n, 1), :] = jnp.mean(kn, axis=0, keepdims=True)
            vt = v_ref[rs, :].T
            vt_s[0, :, rs] = _values_with_ones(vt).astype(BF16)
            vt_s[1, :, rs] = _values_with_ones(pltpu.roll(vt, HEAD_DIM, 0)).astype(BF16)
            return carry

        lax.fori_loop(0, nb, prep_keys, 0)

        def prep_queries(m, carry):
            blk = lax.broadcasted_iota(jnp.int32, (nb, 1), 0)
            blocks = [2 * m, 2 * m + 1]
            rows = [pl.ds(pl.multiple_of(n * MOBA_BLOCK, MOBA_BLOCK), MOBA_BLOCK) for n in blocks]
            qfs = []
            for rs in rows:
                qn = _pair_rms(q_ref[rs, :], gq_ref[...], first) * (HEAD_DIM ** -0.5)
                qfs.append(jnp.concatenate([jnp.where(first, qn, 0.0).T, jnp.where(first, 0.0, qn).T], axis=1))
            gates = [_dot(km_s[...], qf, precision=HIGHEST) for qf in qfs]
            for n, rs, qf, gate in zip(blocks, rows, qfs, gates):
                past = blk < n
                gate = jnp.where(past, gate, -jnp.inf)
                beaten = jnp.zeros((nb, 2 * MOBA_BLOCK), F32)
                for n2 in range(nb):
                    g2 = gate[n2:n2 + 1, :]
                    wins = jnp.logical_or(g2 > gate, jnp.logical_and(g2 == gate, n2 < blk))
                    beaten = beaten + jnp.where(wins, 1.0, 0.0)
                sel = jnp.where(jnp.logical_and(beaten < min(MOBA_TOPK, nb), past), 1.0, 0.0)
                for j in range(2):
                    qt_s[j, :, rs] = qf[:, j * MOBA_BLOCK:(j + 1) * MOBA_BLOCK].astype(BF16)
                    sel_s[j, :, rs] = sel[:, j * MOBA_BLOCK:(j + 1) * MOBA_BLOCK]
            return carry

        lax.fori_loop(0, nb // 2, prep_queries, 0)

    qs = pl.ds(pl.multiple_of(qi * TQ, TQ), TQ)
    qt = jnp.concatenate([qt_s[0, :, qs], qt_s[1, :, qs]], axis=1)
    lane_w = lax.broadcasted_iota(jnp.int32, (1, w), 1)
    in_step = jnp.where(lane_w >= TQ, lane_w - TQ, lane_w)
    cur = qi * per_step + jnp.right_shift(in_step, MOBA_BLOCK.bit_length() - 1)

    def per_query_block(ref):
        return jnp.concatenate([ref[0, :, j * MOBA_BLOCK:(j + 1) * MOBA_BLOCK]
                                for j in range(2) for _ in range(per_step)], axis=1)

    def group_scores(g, diagonal):
        raw = _dot(kn_s[pl.ds(pl.multiple_of(g * KG, KG), KG), :], qt)
        pieces = []
        for i in range(per_group):
            n = g * per_group + i
            s = raw[i * MOBA_BLOCK:(i + 1) * MOBA_BLOCK, :]
            mask = jnp.concatenate([sel_s[0, pl.ds(n, 1), qs], sel_s[1, pl.ds(n, 1), qs]], axis=1) > 0.5
            follows = cur == n + 1
            if diagonal:
                own = cur == n
                kidx, qidx = _key_query_iotas(MOBA_BLOCK, w)
                causal = kidx <= jnp.bitwise_and(qidx, MOBA_BLOCK - 1)
                mask = jnp.logical_or(jnp.logical_and(own, causal), jnp.logical_and(jnp.logical_not(own), mask))
                s = s + jnp.where(own, per_query_block(bown_ref), jnp.where(follows, per_query_block(bprev_ref), 0.0))
            elif i == per_group - 1:
                s = s + jnp.where(follows, per_query_block(bprev_ref), 0.0)
            pieces.append(jnp.where(mask, s, -jnp.inf))
        return jnp.concatenate(pieces, axis=0)

    g0, _ = _diag_group(qi)
    _store_pair(o_ref, *_attend(group_scores(g0, True), g0, lambda g: group_scores(g, False), vt_s, s_s, p_s))


def _moba_attention(p, bown, bprev, gq, gk, b, s):
    assert KG % TQ == 0 and TQ % MOBA_BLOCK == 0 and s % KG == 0
    nq = s // TQ
    nb = s // MOBA_BLOCK
    gq2 = jnp.tile(gq, 2).reshape(1, LANES)
    gk2 = jnp.tile(gk, 2).reshape(1, LANES)
    bias_spec = pl.BlockSpec((1, MOBA_BLOCK, 2 * MOBA_BLOCK), lambda b_, hp, qi: (hp, 0, 0))
    return pl.pallas_call(
        functools.partial(_moba_kernel, seq=s),
        grid=(b, N_PAIRS, nq),
        in_specs=_pair_specs(s, 0, N_PAIRS, 2 * N_PAIRS) + [
            bias_spec, bias_spec,
            pl.BlockSpec((1, LANES), lambda b_, hp, qi: (0, 0)),
            pl.BlockSpec((1, LANES), lambda b_, hp, qi: (0, 0)),
        ],
        out_specs=pl.BlockSpec((TQ, LANES), lambda b_, hp, qi: (b_ * nq + qi, hp)),
        out_shape=jax.ShapeDtypeStruct((b * s, N_PAIRS * LANES), BF16),
        scratch_shapes=[pltpu.VMEM((2, LANES, s), BF16), pltpu.VMEM((s, LANES), BF16),
                        pltpu.VMEM((2, VT_ROWS, s), BF16), pltpu.VMEM((nb, LANES), F32),
                        pltpu.VMEM((2, nb, s), F32)] + _score_scratch(),
        compiler_params=_cparams("parallel", "parallel", "arbitrary"),
        name="moba_attention",
    )(p, p, p, bown, bprev, gq2, gk2)


def _rope(x, cos, sin):
    partner = jnp.where(_lane_iota() < MLA_NOPE + MLA_ROPE // 2,
                        pltpu.roll(x, LANES - MLA_ROPE // 2, 1), pltpu.roll(x, MLA_ROPE // 2, 1))
    return x * cos + partner * sin


def _mla_norm_rope(x, gain, cos, sin):
    ms = jnp.sum(x * x, axis=1, keepdims=True) * (1.0 / MLA_QK)
    return _rope(x * lax.rsqrt(ms + RMS_EPS) * gain, cos, sin)


def _mla_kernel(q_ref, kv_ref, kr_ref, cos_ref, sin_ref, gq_ref, gk_ref, o_ref, qt_s, kn_s, vt_s, s_s, p_s, *, seq):
    qi = pl.program_id(2)
    first = _lane_iota() < HEAD_DIM

    @pl.when(qi == 0)
    def _():
        def prep(r0):
            rs = pl.ds(r0, PREP_ROWS)
            cos, sin = cos_ref[rs, :], sin_ref[rs, :]
            for j in range(2):
                kv = kv_ref[rs, j * LANES:(j + 1) * LANES]
                k = jnp.where(first, kv, kr_ref[rs, :])
                kn_s[j, rs, :] = _mla_norm_rope(k, gk_ref[...], cos, sin).astype(BF16)
                vt_s[j, :, rs] = _values_with_ones(pltpu.roll(kv.T, HEAD_DIM, 0)).astype(BF16)
                q = q_ref[rs, j * LANES:(j + 1) * LANES]
                qt_s[j, :, rs] = _mla_norm_rope(q, gq_ref[...], cos, sin).T.astype(BF16)

        _prep_loop(seq, prep)

    qs = pl.ds(pl.multiple_of(qi * TQ, TQ), TQ)
    qt = [qt_s[0, :, qs], qt_s[1, :, qs]]
    scale = MLA_QK ** -0.5

    def scores(g):
        ks = pl.ds(pl.multiple_of(g * KG, KG), KG)
        return jnp.concatenate([_dot(kn_s[0, ks, :], qt[0]), _dot(kn_s[1, ks, :], qt[1])], axis=1) * scale

    g0, shift = _diag_group(qi)
    diag = _pair_causal(KG, TQ, shift)
    _store_pair(o_ref, *_attend(jnp.where(diag, scores(g0), -jnp.inf), g0, scores, vt_s, s_s, p_s))


def _mla_attention(qd, kvd, p, kr_col, tables, gq, gk, b, s):
    nq = s // TQ
    cos, sin = tables
    gq128 = jnp.zeros((1, LANES), F32).at[0, :MLA_QK].set(gq)
    gk128 = jnp.zeros((1, LANES), F32).at[0, :MLA_QK].set(gk)
    full = pl.BlockSpec((s, LANES), lambda b_, hp, qi: (0, 0))
    gain = pl.BlockSpec((1, LANES), lambda b_, hp, qi: (0, 0))
    return pl.pallas_call(
        functools.partial(_mla_kernel, seq=s),
        grid=(b, N_PAIRS, nq),
        in_specs=[
            pl.BlockSpec((s, 2 * LANES), lambda b_, hp, qi: (b_, hp)),
            pl.BlockSpec((s, 2 * LANES), lambda b_, hp, qi: (b_, hp)),
            pl.BlockSpec((s, LANES), lambda b_, hp, qi: (b_, kr_col)),
            full, full, gain, gain,
        ],
        out_specs=pl.BlockSpec((TQ, LANES), lambda b_, hp, qi: (b_ * nq + qi, hp)),
        out_shape=jax.ShapeDtypeStruct((b * s, N_PAIRS * LANES), BF16),
        scratch_shapes=[pltpu.VMEM((2, LANES, s), BF16), pltpu.VMEM((2, s, LANES), BF16),
                        pltpu.VMEM((2, VT_ROWS, s), BF16)] + _score_scratch(),
        compiler_params=_cparams("parallel", "parallel", "arbitrary"),
        name="mla_attention",
    )(qd, kvd, p, cos, sin, gq128, gk128)


def _rope_tables(s):
    half = MLA_ROPE // 2
    inv_freq = ROPE_BASE ** (-jnp.arange(half, dtype=F32) / half)
    ang = jnp.arange(s, dtype=F32)[:, None] * inv_freq[None, :]
    cos, sin = jnp.cos(ang), jnp.sin(ang)
    z = lambda w: jnp.zeros((s, w), F32)
    cos_t = jnp.concatenate([jnp.ones((s, MLA_NOPE), F32), cos, cos, z(LANES - MLA_QK)], axis=1)
    sin_t = jnp.concatenate([z(MLA_NOPE), -sin, sin, z(LANES - MLA_QK)], axis=1)
    return cos_t, sin_t


def _even_layer(x, b, s, norm, w_in, forget_bias, gq, gk, w_out):
    d = x.shape[1]
    w = HEAD_DIM * N_HEADS
    cuts = [0, w, 2 * w, 3 * w, 3 * w + N_HEADS, 4 * w + N_HEADS, 5 * w + N_HEADS, 6 * w + N_HEADS]
    qa, ka, va, fa, qb, kb, vb = (w_in[:, cuts[i]:cuts[i + 1]] for i in range(7))
    w_perm = jnp.concatenate([qa, ka, va, qb, kb, vb, fa, jnp.zeros((d, LANES - N_HEADS), F32)], axis=1).astype(BF16)
    p = _norm_matmul(x, 0, d, norm, w_perm)
    cdecay = _log_decay(p, 6 * N_PAIRS, forget_bias, b, s)
    out_a = _fox_attention(p, cdecay, gq, gk, b, s)
    out_b = _sb_attention(p, 3 * N_PAIRS, b, s)
    return _proj_residual(out_a, out_b, w_out, x)


def _odd_layer(x, b, s, norm, w_in, gq_moba, gk_moba, q_a_norm, w_q_b, kv_a_norm, w_kv_b, gq_mla, gk_mla,
               w_out, bias_tiles, rope_tables):
    d = x.shape[1]
    n_main = 3 * HEAD_DIM * N_HEADS + MLA_Q_LORA + MLA_KV_LORA
    w_perm = jnp.concatenate([
        w_in[:, :n_main], jnp.zeros((d, MLA_NOPE), F32), w_in[:, n_main:],
        jnp.zeros((d, LANES - MLA_QK), F32)], axis=1).astype(BF16)
    p = _norm_matmul(x, 0, d, norm, w_perm)
    out_c = _moba_attention(p, bias_tiles[0], bias_tiles[1], gq_moba, gk_moba, b, s)
    lat0 = 3 * HEAD_DIM * N_HEADS
    w_q = jnp.pad(w_q_b.reshape(MLA_Q_LORA, N_HEADS, MLA_QK), ((0, 0), (0, 0), (0, LANES - MLA_QK)))
    qd = _norm_matmul(p, lat0 // MLA_Q_LORA, MLA_Q_LORA, q_a_norm, w_q.reshape(MLA_Q_LORA, N_HEADS * LANES).astype(BF16))
    kvd = _norm_matmul(p, (lat0 + MLA_Q_LORA) // MLA_KV_LORA, MLA_KV_LORA, kv_a_norm, w_kv_b.astype(BF16))
    out_d = _mla_attention(qd, kvd, p, (n_main // LANES), rope_tables, gq_mla, gk_mla, b, s)
    return _proj_residual(out_c, out_d, w_out, x)


def kernel(x, ffn_norm, ffn_w_gate_up, ffn_w_down, rel_bias, ev_norm, ev_w_in, ev_forget_bias, ev_fox_q_norm, ev_fox_k_norm, ev_w_out, od_norm, od_w_in, od_moba_q_norm, od_moba_k_norm, od_mla_q_a_norm, od_mla_w_q_b, od_mla_kv_a_norm, od_mla_w_kv_b, od_mla_q_norm, od_mla_k_norm, od_w_out):
    b, s, d = x.shape
    depth = ffn_norm.shape[0]
    bias_tiles = _bias_tiles(rel_bias)
    rope_tables = _rope_tables(s)
    x = x.reshape(b * s, d)
    for layer in range(depth):
        i = layer // 2
        if layer % 2 == 0:
            x = _even_layer(x, b, s, ev_norm[i], ev_w_in[i], ev_forget_bias[i], ev_fox_q_norm[i],
                            ev_fox_k_norm[i], ev_w_out[i])
        else:
            x = _odd_layer(x, b, s, od_norm[i], od_w_in[i], od_moba_q_norm[i], od_moba_k_norm[i],
                           od_mla_q_a_norm[i], od_mla_w_q_b[i], od_mla_kv_a_norm[i], od_mla_w_kv_b[i],
                           od_mla_q_norm[i], od_mla_k_norm[i], od_w_out[i], bias_tiles, rope_tables)
        x = _ffn(x, ffn_norm[layer], ffn_w_gate_up[layer], ffn_w_down[layer])
    return x.reshape(b, s, d)
```

```python
import functools
import math

import jax
import jax.numpy as jnp
from jax import lax
from jax.experimental import pallas as pl
from jax.experimental.pallas import tpu as pltpu

F32 = jnp.float32
BF16 = jnp.bfloat16

HEAD_DIM = 64
N_HEADS = 8
N_PAIRS = N_HEADS // 2
LANES = 128
MOBA_BLOCK = 256
MOBA_TOPK = 3
MLA_Q_LORA = 256
MLA_KV_LORA = 128
MLA_NOPE = 64
MLA_ROPE = 32
MLA_QK = MLA_NOPE + MLA_ROPE
ROPE_BASE = 10000.0
REL_BUCKETS = 32
REL_MAX_EXACT = 16
REL_MAX_DISTANCE = 128
RMS_EPS = 1e-6
TQ = 256
KG = 512
SB_TQ = 256
TK = 256
VT_ROWS = 80
PREP_ROWS = 512
DEAD_LOG = -110.0
VMEM_LIMIT = 56 * 1024 * 1024
HIGHEST = lax.Precision.HIGHEST


def _cparams(*sem):
    return pltpu.CompilerParams(dimension_semantics=sem, vmem_limit_bytes=VMEM_LIMIT)


def _dot(a, b, **kw):
    return jnp.dot(a, b, preferred_element_type=F32, **kw)


def _log_sigmoid(z):
    return jnp.minimum(z, 0.0) - jnp.log1p(jnp.exp(-jnp.abs(z)))


def _lane_iota():
    return lax.broadcasted_iota(jnp.int32, (1, LANES), 1)


def _row_iota():
    return lax.broadcasted_iota(jnp.int32, (LANES, 1), 0)


def _pair_rms(x, gain, first):
    sq = x * x
    s0 = jnp.sum(jnp.where(first, sq, 0.0), axis=1, keepdims=True)
    s1 = jnp.sum(jnp.where(first, 0.0, sq), axis=1, keepdims=True)
    ms = jnp.where(first, s0, s1) * (1.0 / HEAD_DIM)
    return x * lax.rsqrt(ms + RMS_EPS) * gain


def _split3(c):
    hi = c.astype(BF16).astype(F32)
    mid = (c - hi).astype(BF16).astype(F32)
    return hi, mid, c - hi - mid


def _norm_matmul_kernel(x_ref, g_ref, w_ref, o_ref, *, n_chunk):
    x = x_ref[...]
    ms = jnp.mean(x * x, axis=-1, keepdims=True)
    h = (x * lax.rsqrt(ms + RMS_EPS) * g_ref[...]).astype(BF16)
    n = o_ref.shape[1]
    for c0 in range(0, n, n_chunk):
        c1 = min(n, c0 + n_chunk)
        o_ref[:, c0:c1] = _dot(h, w_ref[:, c0:c1])


def _norm_matmul(xw, col_blk, k, gain, w, tm=256, n_chunk=512):
    m = xw.shape[0]
    n = w.shape[1]
    return pl.pallas_call(
        functools.partial(_norm_matmul_kernel, n_chunk=n_chunk),
        grid=(m // tm,),
        in_specs=[
            pl.BlockSpec((tm, k), lambda i: (i, col_blk)),
            pl.BlockSpec((1, k), lambda i: (0, 0)),
            pl.BlockSpec((k, n), lambda i: (0, 0)),
        ],
        out_specs=pl.BlockSpec((tm, n), lambda i: (i, 0)),
        out_shape=jax.ShapeDtypeStruct((m, n), F32),
        compiler_params=_cparams("parallel"),
        name="norm_matmul",
    )(xw, gain.reshape(1, k), w)


def _mix_ffn_kernel(a_ref, b_ref, wa_ref, wb_ref, r_ref, g_ref, wg_ref, wu_ref, wd_ref, o_ref, x_s, h_s, acc_s):
    c = pl.program_id(1)

    @pl.when(c == 0)
    def _():
        x = r_ref[...] + (_dot(a_ref[...], wa_ref[...]) + _dot(b_ref[...], wb_ref[...]))
        x_s[...] = x
        ms = jnp.mean(x * x, axis=-1, keepdims=True)
        h_s[...] = (x * lax.rsqrt(ms + RMS_EPS) * g_ref[...]).astype(BF16)
        acc_s[...] = jnp.zeros_like(acc_s)

    h = h_s[...]
    g = _dot(h, wg_ref[...])
    u = _dot(h, wu_ref[...])
    act = (g * jax.nn.sigmoid(g) * u).astype(BF16)
    acc_s[...] += _dot(act, wd_ref[...])

    @pl.when(c == pl.num_programs(1) - 1)
    def _():
        o_ref[...] = x_s[...] + acc_s[...]


def _mix_ffn(a, b, w_out, res, gain, w_gate_up, w_down, tm=512, n_ff_chunks=2):
    m, d = res.shape
    ka = a.shape[1]
    d_ff = w_down.shape[0]
    tf = d_ff // n_ff_chunks
    assert tf * n_ff_chunks == d_ff and tf % LANES == 0
    wo = w_out.astype(BF16)
    wgu = w_gate_up.astype(BF16)
    wd = w_down.astype(BF16)
    return pl.pallas_call(
        _mix_ffn_kernel,
        grid=(m // tm, n_ff_chunks),
        in_specs=[
            pl.BlockSpec((tm, ka), lambda i, c: (i, 0)),
            pl.BlockSpec((tm, ka), lambda i, c: (i, 0)),
            pl.BlockSpec((ka, d), lambda i, c: (0, 0)),
            pl.BlockSpec((ka, d), lambda i, c: (1, 0)),
            pl.BlockSpec((tm, d), lambda i, c: (i, 0)),
            pl.BlockSpec((1, d), lambda i, c: (0, 0)),
            pl.BlockSpec((d, tf), lambda i, c: (0, c)),
            pl.BlockSpec((d, tf), lambda i, c: (0, n_ff_chunks + c)),
            pl.BlockSpec((tf, d), lambda i, c: (c, 0)),
        ],
        out_specs=pl.BlockSpec((tm, d), lambda i, c: (i, 0)),
        out_shape=jax.ShapeDtypeStruct((m, d), F32),
        scratch_shapes=[pltpu.VMEM((tm, d), F32), pltpu.VMEM((tm, d), BF16), pltpu.VMEM((tm, d), F32)],
        compiler_params=_cparams("parallel", "arbitrary"),
        name="mix_swiglu_ffn",
    )(a, b, wo, wo, res, gain.reshape(1, d), wgu, wgu, wd)


def _decay_kernel(f_ref, b_ref, c_ref, carry_s):
    @pl.when(pl.program_id(1) == 0)
    def _():
        carry_s[...] = jnp.zeros_like(carry_s)

    ts = f_ref.shape[0]
    logf = _log_sigmoid(f_ref[...] + b_ref[...])
    r = lax.broadcasted_iota(jnp.int32, (ts, ts), 0)
    c = lax.broadcasted_iota(jnp.int32, (ts, ts), 1)
    tri = (c <= r).astype(F32)
    cs = _dot(tri, logf, precision=HIGHEST) + carry_s[...]
    carry_s[...] = cs[ts - 1:ts, :]
    c_ref[...] = cs[:, 0:N_HEADS]


def _log_decay(p, col_blk, bias, b, s, ts=256):
    ns = s // ts
    bias128 = jnp.zeros((1, LANES), F32).at[0, :N_HEADS].set(bias)
    return pl.pallas_call(
        _decay_kernel,
        grid=(b, ns),
        in_specs=[
            pl.BlockSpec((ts, LANES), lambda bi, si: (bi * ns + si, col_blk)),
            pl.BlockSpec((1, LANES), lambda bi, si: (0, 0)),
        ],
        out_specs=pl.BlockSpec((ts, N_HEADS), lambda bi, si: (bi * ns + si, 0)),
        out_shape=jax.ShapeDtypeStruct((b * s, N_HEADS), F32),
        scratch_shapes=[pltpu.VMEM((1, LANES), F32)],
        compiler_params=_cparams("parallel", "arbitrary"),
        name="fox_log_decay",
    )(p, bias128)


def _attend(first_scores, first_group, scores_fn, vt_s, s_s, p_s, alive_fn=None):
    w = 2 * TQ
    s_s[0] = first_scores
    p_s[0] = jnp.zeros((KG, w), BF16)

    def values(g, pr, al, a0, a1):
        k0 = pl.multiple_of(g * KG, KG)
        a0 = al[:, 0:TQ] * a0 + _dot(vt_s[0, :, pl.ds(k0, KG)], pr[:, 0:TQ])
        a1 = al[:, TQ:w] * a1 + _dot(vt_s[1, :, pl.ds(k0, KG)], pr[:, TQ:w])
        return a0, a1

    def softmax(s, m):
        m_new = jnp.maximum(m, jnp.max(s, axis=0, keepdims=True))
        return m_new, jnp.exp(m - m_new), jnp.exp(s - m_new).astype(BF16)

    def cond(carry):
        return jnp.logical_and(carry[0] < first_group, carry[1] > 0)

    def body(carry):
        i, _, m, al, a0, a1 = carry
        par = lax.rem(i, 2)
        g = first_group - i
        pr_prev = p_s[par]
        s = s_s[par]
        a0, a1 = values(jnp.minimum(g + 1, first_group), pr_prev, al, a0, a1)
        s_next = scores_fn(g - 1)
        m_new, alpha, pr = softmax(s, m)
        s_s[1 - par] = s_next
        p_s[1 - par] = pr
        alive = jnp.int32(1) if alive_fn is None else alive_fn(g - 1, m_new).astype(jnp.int32)
        return i + 1, alive, m_new, alpha, a0, a1

    init = (jnp.int32(0), jnp.int32(1), jnp.full((1, w), -jnp.inf, F32), jnp.ones((1, w), F32),
            jnp.zeros((VT_ROWS, TQ), F32), jnp.zeros((VT_ROWS, TQ), F32))
    n, _, m, al, a0, a1 = lax.while_loop(cond, body, init)
    par = lax.rem(n, 2)
    g = first_group - n
    a0, a1 = values(jnp.minimum(g + 1, first_group), p_s[par], al, a0, a1)
    _, alpha, pr = softmax(s_s[par], m)
    a0, a1 = values(g, pr, alpha, a0, a1)
    return (a0[0:HEAD_DIM, :] / a0[HEAD_DIM:HEAD_DIM + 1, :],
            a1[0:HEAD_DIM, :] / a1[HEAD_DIM:HEAD_DIM + 1, :])


def _score_scratch():
    return [pltpu.VMEM((2, KG, 2 * TQ), F32), pltpu.VMEM((2, KG, 2 * TQ), BF16)]


def _key_query_iotas(nk, nq):
    k = lax.broadcasted_iota(jnp.int32, (nk, nq), 0)
    q = lax.broadcasted_iota(jnp.int32, (nk, nq), 1)
    return k, q


def _pair_causal(nk, tq, shift=0, strict=False):
    kidx, qidx = _key_query_iotas(nk, 2 * tq)
    qidx = jnp.where(qidx >= tq, qidx - tq, qidx) + shift
    return kidx < qidx if strict else kidx <= qidx


def _values_with_ones(vt):
    return jnp.where(_row_iota() < HEAD_DIM, vt, 1.0)[0:VT_ROWS, :]


def _prep_loop(n_rows, fn):
    def body(i, carry):
        fn(pl.multiple_of(i * PREP_ROWS, PREP_ROWS))
        return carry

    lax.fori_loop(0, n_rows // PREP_ROWS, body, 0)


def _pair_specs(s, qcol, kcol, vcol):
    return [pl.BlockSpec((s, LANES), lambda b, hp, qi, col=col: (b, col + hp)) for col in (qcol, kcol, vcol)]


def _diag_group(qi):
    g0 = (qi * TQ + TQ - 1) // KG
    return g0, qi * TQ - g0 * KG


def _store_pair(o_ref, out0, out1):
    o_ref[...] = jnp.concatenate([out0, out1], axis=0).T.astype(o_ref.dtype)


def _fox_kernel(q_ref, k_ref, v_ref, c_ref, gq_ref, gk_ref, o_ref,
                qt_s, qb_s, ka_s, vt_s, kmax_s, s_s, p_s, *, seq):
    hp = pl.program_id(1)
    qi = pl.program_id(2)
    lane = _lane_iota()
    first = lane < HEAD_DIM
    head_lane = lax.broadcasted_iota(jnp.int32, (1, N_HEADS), 1)
    second = lax.broadcasted_iota(jnp.int32, (1, 2 * TQ), 1) >= TQ

    def augment(x, c, key_side):
        hi, mid, lo = _split3(c)
        sgn = -1.0 if key_side else 1.0
        c0 = HEAD_DIM + (3 if key_side else 0)
        one0 = HEAD_DIM + (0 if key_side else 3)
        out = jnp.where(first, x, 0.0)
        out = jnp.where(jnp.logical_and(lane >= one0, lane < one0 + 3), 1.0, out)
        for i, t in enumerate((hi, mid, lo)):
            out = jnp.where(lane == c0 + i, sgn * t, out)
        return out

    def head_col(cc, h):
        return jnp.sum(jnp.where(head_lane == h, cc, 0.0), axis=1, keepdims=True)

    @pl.when(qi == 0)
    def _():
        kmax_s[...] = jnp.zeros_like(kmax_s)

        def prep(r0):
            rs = pl.ds(r0, PREP_ROWS)
            kn = _pair_rms(k_ref[rs, :], gk_ref[...], first)
            qn = _pair_rms(q_ref[rs, :], gq_ref[...], first) * (HEAD_DIM ** -0.5)
            vt = v_ref[rs, :].T
            cc = c_ref[rs, :]
            for j in range(2):
                c = head_col(cc, 2 * hp + j)
                kj = kn if j == 0 else pltpu.roll(kn, HEAD_DIM, 1)
                ka_s[j, rs, :] = augment(kj, c, True).astype(BF16)
                vj = vt if j == 0 else pltpu.roll(vt, HEAD_DIM, 0)
                vt_s[j, :, rs] = _values_with_ones(vj).astype(BF16)
                sq = jnp.sum(jnp.where(first, kj * kj, 0.0), axis=1, keepdims=True)
                kmax_s[j] = jnp.maximum(kmax_s[j], jnp.max(sq, axis=0, keepdims=True))
                qa = augment(qn if j == 0 else pltpu.roll(qn, HEAD_DIM, 1), c, False).T
                qt_s[j, :, rs] = qa.astype(BF16)
                qb_s[j, 0:1, rs] = jnp.sqrt(jnp.sum(qa[0:HEAD_DIM, :] * qa[0:HEAD_DIM, :], axis=0, keepdims=True))
                qb_s[j, 1:2, rs] = jnp.sum(qa[HEAD_DIM:HEAD_DIM + 3, :], axis=0, keepdims=True)

        _prep_loop(seq, prep)

    qs = pl.ds(pl.multiple_of(qi * TQ, TQ), TQ)
    qt = [qt_s[0, :, qs], qt_s[1, :, qs]]
    score_bound = jnp.concatenate(
        [1.02 * qb_s[j, 0:1, qs] * jnp.sqrt(kmax_s[j][:, 0:1]) + qb_s[j, 1:2, qs] for j in range(2)], axis=1)

    def scores(g):
        ks = pl.ds(pl.multiple_of(g * KG, KG), KG)
        return jnp.concatenate([_dot(ka_s[0, ks, :], qt[0]), _dot(ka_s[1, ks, :], qt[1])], axis=1)

    def alive(g, m):
        row = c_ref[pl.ds((g + 1) * KG - 1, 1), :]
        c0 = jnp.sum(jnp.where(head_lane == 2 * hp, row, 0.0), axis=1, keepdims=True)
        c1 = jnp.sum(jnp.where(head_lane == 2 * hp + 1, row, 0.0), axis=1, keepdims=True)
        return jnp.max(score_bound - jnp.where(second, c1, c0) - m) > DEAD_LOG

    g0, shift = _diag_group(qi)
    diag = _pair_causal(KG, TQ, shift)
    _store_pair(o_ref, *_attend(jnp.where(diag, scores(g0), -jnp.inf), g0, scores, vt_s, s_s, p_s, alive))


def _fox_attention(p, cdecay, gq, gk, b, s):
    nq = s // TQ
    gq2 = jnp.tile(gq, 2).reshape(1, LANES)
    gk2 = jnp.tile(gk, 2).reshape(1, LANES)
    return pl.pallas_call(
        functools.partial(_fox_kernel, seq=s),
        grid=(b, N_PAIRS, nq),
        in_specs=_pair_specs(s, 0, N_PAIRS, 2 * N_PAIRS) + [
            pl.BlockSpec((s, N_HEADS), lambda b_, hp, qi: (b_, 0)),
            pl.BlockSpec((1, LANES), lambda b_, hp, qi: (0, 0)),
            pl.BlockSpec((1, LANES), lambda b_, hp, qi: (0, 0)),
        ],
        out_specs=pl.BlockSpec((TQ, LANES), lambda b_, hp, qi: (b_ * nq + qi, hp)),
        out_shape=jax.ShapeDtypeStruct((b * s, N_PAIRS * LANES), BF16),
        scratch_shapes=[pltpu.VMEM((2, LANES, s), BF16), pltpu.VMEM((2, 8, s), F32),
                        pltpu.VMEM((2, s, LANES), BF16), pltpu.VMEM((2, VT_ROWS, s), BF16),
                        pltpu.VMEM((2, 1, LANES), F32)] + _score_scratch(),
        compiler_params=_cparams("parallel", "parallel", "arbitrary"),
        name="fox_attention",
    )(p, p, p, cdecay, gq2, gk2)


def _sb_kernel(q_ref, k_ref, v_ref, u_ref, o_ref, qt_s, kb_s, vt_s, *, seq):
    qi = pl.program_id(2)
    first = _lane_iota() < HEAD_DIM
    w = 2 * SB_TQ

    @pl.when(qi == 0)
    def _():
        def prep(r0):
            rs = pl.ds(r0, PREP_ROWS)
            kb_s[rs, :] = k_ref[rs, :].astype(BF16)
            vt_s[:, rs] = v_ref[rs, :].T.astype(BF16)
            q = q_ref[rs, :] * (HEAD_DIM ** -0.5)
            qt_s[0, :, rs] = jnp.where(first, q, 0.0).T.astype(BF16)
            qt_s[1, :, rs] = jnp.where(first, 0.0, q).T.astype(BF16)

        _prep_loop(seq, prep)

    qs = pl.ds(pl.multiple_of(qi * SB_TQ, SB_TQ), SB_TQ)
    qt = jnp.concatenate([qt_s[0, :, qs], qt_s[1, :, qs]], axis=1)

    def chunk(k0, nk, drop, acc, strict):
        z = _dot(kb_s[pl.ds(k0, nk), :], qt)
        softplus = jnp.maximum(z, 0.0) + jnp.log(1.0 + jnp.exp(-jnp.abs(z)))
        log_b = z - softplus
        if strict is not None:
            softplus = jnp.where(strict, softplus, 0.0)
        hi = softplus.astype(BF16)
        lo = (softplus - hi.astype(F32)).astype(BF16)
        later = u_ref[0:nk, 0:nk]
        wgt = jnp.exp(log_b - (drop + (_dot(later, hi) + _dot(later, lo))))
        if strict is not None:
            wgt = jnp.where(strict, wgt, 0.0)
        acc = acc + _dot(vt_s[:, pl.ds(k0, nk)], wgt.astype(BF16))
        return drop + jnp.sum(softplus, axis=0, keepdims=True), acc

    k0 = pl.multiple_of(jnp.maximum(qi - 1, 0) * TK, TK)
    strict = _pair_causal(2 * TK, SB_TQ, qi * SB_TQ - k0, strict=True)
    rsum, acc = chunk(k0, 2 * TK, jnp.zeros((1, w), F32), jnp.zeros((LANES, w), F32), strict)

    def cond(c):
        return jnp.logical_and(c[0] >= 0, c[1] > 0)

    def alive(drop):
        return (jnp.min(drop) < -DEAD_LOG).astype(jnp.int32)

    def body(c):
        rs, ac = chunk(pl.multiple_of(c[0] * TK, TK), TK, c[2], c[3], None)
        return c[0] - 1, alive(rs), rs, ac

    _, _, _, acc = lax.while_loop(cond, body, (qi - 2, alive(rsum), rsum, acc))
    _store_pair(o_ref, acc[0:HEAD_DIM, 0:SB_TQ], acc[HEAD_DIM:LANES, SB_TQ:w])


def _sb_attention(p, qcol, b, s):
    assert SB_TQ == TK
    nq = s // SB_TQ
    r = lax.broadcasted_iota(jnp.int32, (2 * TK, 2 * TK), 0)
    c = lax.broadcasted_iota(jnp.int32, (2 * TK, 2 * TK), 1)
    later = (c > r).astype(BF16)
    return pl.pallas_call(
        functools.partial(_sb_kernel, seq=s),
        grid=(b, N_PAIRS, nq),
        in_specs=_pair_specs(s, qcol, qcol + N_PAIRS, qcol + 2 * N_PAIRS) + [
            pl.BlockSpec((2 * TK, 2 * TK), lambda b_, hp, qi: (0, 0)),
        ],
        out_specs=pl.BlockSpec((SB_TQ, LANES), lambda b_, hp, qi: (b_ * nq + qi, hp)),
        out_shape=jax.ShapeDtypeStruct((b * s, N_PAIRS * LANES), BF16),
        scratch_shapes=[pltpu.VMEM((2, LANES, s), BF16), pltpu.VMEM((s, LANES), BF16), pltpu.VMEM((LANES, s), BF16)],
        compiler_params=_cparams("parallel", "parallel", "arbitrary"),
        name="stick_breaking_attention",
    )(p, p, p, later)


def _t5_bucket(rel):
    n = jnp.maximum(rel, 0)
    nf = jnp.maximum(n, 1).astype(F32)
    large = REL_MAX_EXACT + (jnp.log(nf / REL_MAX_EXACT) / math.log(REL_MAX_DISTANCE / REL_MAX_EXACT)
                             * (REL_BUCKETS - REL_MAX_EXACT)).astype(jnp.int32)
    large = jnp.minimum(large, REL_BUCKETS - 1)
    return jnp.where(n < REL_MAX_EXACT, n, large)


def _bias_tile_kernel(rb_ref, own_ref, prev_ref):
    hp = pl.program_id(0)
    kidx, qidx = _key_query_iotas(MOBA_BLOCK, MOBA_BLOCK)
    for off, ref in ((0, own_ref), (MOBA_BLOCK, prev_ref)):
        bucket = _t5_bucket(qidx - kidx + off)
        for j in range(2):
            out = jnp.zeros((MOBA_BLOCK, MOBA_BLOCK), F32)
            for bk in range(REL_BUCKETS):
                out = jnp.where(bucket == bk, rb_ref[2 * hp + j, bk], out)
            ref[0, :, j * MOBA_BLOCK:(j + 1) * MOBA_BLOCK] = out - rb_ref[2 * hp + j, REL_BUCKETS - 1]


def _bias_tiles(rel_bias):
    shape = jax.ShapeDtypeStruct((N_PAIRS, MOBA_BLOCK, 2 * MOBA_BLOCK), F32)
    spec = pl.BlockSpec((1, MOBA_BLOCK, 2 * MOBA_BLOCK), lambda h: (h, 0, 0))
    return pl.pallas_call(
        _bias_tile_kernel,
        grid=(N_PAIRS,),
        in_specs=[pl.BlockSpec(memory_space=pltpu.SMEM)],
        out_specs=[spec, spec],
        out_shape=[shape, shape],
        compiler_params=_cparams("arbitrary"),
        name="moba_bias_tiles",
    )(rel_bias)


def _moba_kernel(q_ref, k_ref, v_ref, bown_ref, bprev_ref, gq_ref, gk_ref, o_ref,
                 qt_s, kn_s, vt_s, km_s, sel_s, s_s, p_s, *, seq):
    qi = pl.program_id(2)
    nb = seq // MOBA_BLOCK
    first = _lane_iota() < HEAD_DIM
    w = 2 * TQ
    per_step = TQ // MOBA_BLOCK
    per_group = KG // MOBA_BLOCK

    @pl.when(qi == 0)
    def _():
        def prep_keys(n, carry):
            rs = pl.ds(pl.multiple_of(n * MOBA_BLOCK, MOBA_BLOCK), MOBA_BLOCK)
            kn = _pair_rms(k_ref[rs, :], gk_ref[...], first)
            kn_s[rs, :] = kn.astype(BF16)
            km_s[pl.ds(n, 1), :] = jnp.mean(kn, axis=0, keepdims=True)
            vt = v_ref[rs, :].T
            vt_s[0, :, rs] = _values_with_ones(vt).astype(BF16)
            vt_s[1, :, rs] = _values_with_ones(pltpu.roll(vt, HEAD_DIM, 0)).astype(BF16)
            return carry

        lax.fori_loop(0, nb, prep_keys, 0)

        def prep_queries(m, carry):
            blk = lax.broadcasted_iota(jnp.int32, (nb, 1), 0)
            blocks = [2 * m, 2 * m + 1]
            rows = [pl.ds(pl.multiple_of(n * MOBA_BLOCK, MOBA_BLOCK), MOBA_BLOCK) for n in blocks]
            qfs = []
            for rs in rows:
                qn = _pair_rms(q_ref[rs, :], gq_ref[...], first) * (HEAD_DIM ** -0.5)
                qfs.append(jnp.concatenate([jnp.where(first, qn, 0.0).T, jnp.where(first, 0.0, qn).T], axis=1))
            gates = [_dot(km_s[...], qf, precision=HIGHEST) for qf in qfs]
            for n, rs, qf, gate in zip(blocks, rows, qfs, gates):
                past = blk < n
                gate = jnp.where(past, gate, -jnp.inf)
                beaten = jnp.zeros((nb, 2 * MOBA_BLOCK), F32)
                for n2 in range(nb):
                    g2 = gate[n2:n2 + 1, :]
                    wins = jnp.logical_or(g2 > gate, jnp.logical_and(g2 == gate, n2 < blk))
                    beaten = beaten + jnp.where(wins, 1.0, 0.0)
                sel = jnp.where(jnp.logical_and(beaten < min(MOBA_TOPK, nb), past), 1.0, 0.0)
                for j in range(2):
                    qt_s[j, :, rs] = qf[:, j * MOBA_BLOCK:(j + 1) * MOBA_BLOCK].astype(BF16)
                    sel_s[j, :, rs] = sel[:, j * MOBA_BLOCK:(j + 1) * MOBA_BLOCK]
            return carry

        lax.fori_loop(0, nb // 2, prep_queries, 0)

    qs = pl.ds(pl.multiple_of(qi * TQ, TQ), TQ)
    qt = jnp.concatenate([qt_s[0, :, qs], qt_s[1, :, qs]], axis=1)
    lane_w = lax.broadcasted_iota(jnp.int32, (1, w), 1)
    in_step = jnp.where(lane_w >= TQ, lane_w - TQ, lane_w)
    cur = qi * per_step + jnp.right_shift(in_step, MOBA_BLOCK.bit_length() - 1)

    def per_query_block(ref):
        return jnp.concatenate([ref[0, :, j * MOBA_BLOCK:(j + 1) * MOBA_BLOCK]
                                for j in range(2) for _ in range(per_step)], axis=1)

    def group_scores(g, diagonal):
        raw = _dot(kn_s[pl.ds(pl.multiple_of(g * KG, KG), KG), :], qt)
        pieces = []
        for i in range(per_group):
            n = g * per_group + i
            s = raw[i * MOBA_BLOCK:(i + 1) * MOBA_BLOCK, :]
            mask = jnp.concatenate([sel_s[0, pl.ds(n, 1), qs], sel_s[1, pl.ds(n, 1), qs]], axis=1) > 0.5
            follows = cur == n + 1
            if diagonal:
                own = cur == n
                kidx, qidx = _key_query_iotas(MOBA_BLOCK, w)
                causal = kidx <= jnp.bitwise_and(qidx, MOBA_BLOCK - 1)
                mask = jnp.logical_or(jnp.logical_and(own, causal), jnp.logical_and(jnp.logical_not(own), mask))
                s = s + jnp.where(own, per_query_block(bown_ref), jnp.where(follows, per_query_block(bprev_ref), 0.0))
            elif i == per_group - 1:
                s = s + jnp.where(follows, per_query_block(bprev_ref), 0.0)
            pieces.append(jnp.where(mask, s, -jnp.inf))
        return jnp.concatenate(pieces, axis=0)

    g0, _ = _diag_group(qi)
    _store_pair(o_ref, *_attend(group_scores(g0, True), g0, lambda g: group_scores(g, False), vt_s, s_s, p_s))


def _moba_attention(p, bown, bprev, gq, gk, b, s):
    assert KG % TQ == 0 and TQ % MOBA_BLOCK == 0 and s % KG == 0
    nq = s // TQ
    nb = s // MOBA_BLOCK
    gq2 = jnp.tile(gq, 2).reshape(1, LANES)
    gk2 = jnp.tile(gk, 2).reshape(1, LANES)
    bias_spec = pl.BlockSpec((1, MOBA_BLOCK, 2 * MOBA_BLOCK), lambda b_, hp, qi: (hp, 0, 0))
    return pl.pallas_call(
        functools.partial(_moba_kernel, seq=s),
        grid=(b, N_PAIRS, nq),
        in_specs=_pair_specs(s, 0, N_PAIRS, 2 * N_PAIRS) + [
            bias_spec, bias_spec,
            pl.BlockSpec((1, LANES), lambda b_, hp, qi: (0, 0)),
            pl.BlockSpec((1, LANES), lambda b_, hp, qi: (0, 0)),
        ],
        out_specs=pl.BlockSpec((TQ, LANES), lambda b_, hp, qi: (b_ * nq + qi, hp)),
        out_shape=jax.ShapeDtypeStruct((b * s, N_PAIRS * LANES), BF16),
        scratch_shapes=[pltpu.VMEM((2, LANES, s), BF16), pltpu.VMEM((s, LANES), BF16),
                        pltpu.VMEM((2, VT_ROWS, s), BF16), pltpu.VMEM((nb, LANES), F32),
                        pltpu.VMEM((2, nb, s), F32)] + _score_scratch(),
        compiler_params=_cparams("parallel", "parallel", "arbitrary"),
        name="moba_attention",
    )(p, p, p, bown, bprev, gq2, gk2)


def _rope(x, cos, sin):
    partner = jnp.where(_lane_iota() < MLA_NOPE + MLA_ROPE // 2,
                        pltpu.roll(x, LANES - MLA_ROPE // 2, 1), pltpu.roll(x, MLA_ROPE // 2, 1))
    return x * cos + partner * sin


def _mla_norm_rope(x, gain, cos, sin):
    ms = jnp.sum(x * x, axis=1, keepdims=True) * (1.0 / MLA_QK)
    return _rope(x * lax.rsqrt(ms + RMS_EPS) * gain, cos, sin)


def _mla_kernel(q_ref, kv_ref, kr_ref, cos_ref, sin_ref, gq_ref, gk_ref, o_ref, qt_s, kn_s, vt_s, s_s, p_s, *, seq):
    qi = pl.program_id(2)
    first = _lane_iota() < HEAD_DIM

    @pl.when(qi == 0)
    def _():
        def prep(r0):
            rs = pl.ds(r0, PREP_ROWS)
            cos, sin = cos_ref[rs, :], sin_ref[rs, :]
            for j in range(2):
                kv = kv_ref[rs, j * LANES:(j + 1) * LANES]
                k = jnp.where(first, kv, kr_ref[rs, :])
                kn_s[j, rs, :] = _mla_norm_rope(k, gk_ref[...], cos, sin).astype(BF16)
                vt_s[j, :, rs] = _values_with_ones(pltpu.roll(kv.T, HEAD_DIM, 0)).astype(BF16)
                q = q_ref[rs, j * LANES:(j + 1) * LANES]
                qt_s[j, :, rs] = _mla_norm_rope(q, gq_ref[...], cos, sin).T.astype(BF16)

        _prep_loop(seq, prep)

    qs = pl.ds(pl.multiple_of(qi * TQ, TQ), TQ)
    qt = [qt_s[0, :, qs], qt_s[1, :, qs]]
    scale = MLA_QK ** -0.5

    def scores(g):
        ks = pl.ds(pl.multiple_of(g * KG, KG), KG)
        return jnp.concatenate([_dot(kn_s[0, ks, :], qt[0]), _dot(kn_s[1, ks, :], qt[1])], axis=1) * scale

    g0, shift = _diag_group(qi)
    diag = _pair_causal(KG, TQ, shift)
    _store_pair(o_ref, *_attend(jnp.where(diag, scores(g0), -jnp.inf), g0, scores, vt_s, s_s, p_s))


def _mla_attention(qd, kvd, p, kr_col, tables, gq, gk, b, s):
    nq = s // TQ
    cos, sin = tables
    gq128 = jnp.zeros((1, LANES), F32).at[0, :MLA_QK].set(gq)
    gk128 = jnp.zeros((1, LANES), F32).at[0, :MLA_QK].set(gk)
    full = pl.BlockSpec((s, LANES), lambda b_, hp, qi: (0, 0))
    gain = pl.BlockSpec((1, LANES), lambda b_, hp, qi: (0, 0))
    return pl.pallas_call(
        functools.partial(_mla_kernel, seq=s),
        grid=(b, N_PAIRS, nq),
        in_specs=[
            pl.BlockSpec((s, 2 * LANES), lambda b_, hp, qi: (b_, hp)),
            pl.BlockSpec((s, 2 * LANES), lambda b_, hp, qi: (b_, hp)),
            pl.BlockSpec((s, LANES), lambda b_, hp, qi: (b_, kr_col)),
            full, full, gain, gain,
        ],
        out_specs=pl.BlockSpec((TQ, LANES), lambda b_, hp, qi: (b_ * nq + qi, hp)),
        out_shape=jax.ShapeDtypeStruct((b * s, N_PAIRS * LANES), BF16),
        scratch_shapes=[pltpu.VMEM((2, LANES, s), BF16), pltpu.VMEM((2, s, LANES), BF16),
                        pltpu.VMEM((2, VT_ROWS, s), BF16)] + _score_scratch(),
        compiler_params=_cparams("parallel", "parallel", "arbitrary"),
        name="mla_attention",
    )(qd, kvd, p, cos, sin, gq128, gk128)


def _rope_tables(s):
    half = MLA_ROPE // 2
    inv_freq = ROPE_BASE ** (-jnp.arange(half, dtype=F32) / half)
    ang = jnp.arange(s, dtype=F32)[:, None] * inv_freq[None, :]
    cos, sin = jnp.cos(ang), jnp.sin(ang)
    z = lambda w: jnp.zeros((s, w), F32)
    cos_t = jnp.concatenate([jnp.ones((s, MLA_NOPE), F32), cos, cos, z(LANES - MLA_QK)], axis=1)
    sin_t = jnp.concatenate([z(MLA_NOPE), -sin, sin, z(LANES - MLA_QK)], axis=1)
    return cos_t, sin_t


def _even_mixer(x, b, s, norm, w_in, forget_bias, gq, gk):
    d = x.shape[1]
    w = HEAD_DIM * N_HEADS
    cuts = [0, w, 2 * w, 3 * w, 3 * w + N_HEADS, 4 * w + N_HEADS, 5 * w + N_HEADS, 6 * w + N_HEADS]
    qa, ka, va, fa, qb, kb, vb = (w_in[:, cuts[i]:cuts[i + 1]] for i in range(7))
    w_perm = jnp.concatenate([qa, ka, va, qb, kb, vb, fa, jnp.zeros((d, LANES - N_HEADS), F32)], axis=1).astype(BF16)
    p = _norm_matmul(x, 0, d, norm, w_perm)
    cdecay = _log_decay(p, 6 * N_PAIRS, forget_bias, b, s)
    out_a = _fox_attention(p, cdecay, gq, gk, b, s)
    out_b = _sb_attention(p, 3 * N_PAIRS, b, s)
    return out_a, out_b


def _odd_mixer(x, b, s, norm, w_in, gq_moba, gk_moba, q_a_norm, w_q_b, kv_a_norm, w_kv_b, gq_mla, gk_mla,
               bias_tiles, rope_tables):
    d = x.shape[1]
    n_main = 3 * HEAD_DIM * N_HEADS + MLA_Q_LORA + MLA_KV_LORA
    w_perm = jnp.concatenate([
        w_in[:, :n_main], jnp.zeros((d, MLA_NOPE), F32), w_in[:, n_main:],
        jnp.zeros((d, LANES - MLA_QK), F32)], axis=1).astype(BF16)
    p = _norm_matmul(x, 0, d, norm, w_perm)
    out_c = _moba_attention(p, bias_tiles[0], bias_tiles[1], gq_moba, gk_moba, b, s)
    lat0 = 3 * HEAD_DIM * N_HEADS
    w_q = jnp.pad(w_q_b.reshape(MLA_Q_LORA, N_HEADS, MLA_QK), ((0, 0), (0, 0), (0, LANES - MLA_QK)))
    qd = _norm_matmul(p, lat0 // MLA_Q_LORA, MLA_Q_LORA, q_a_norm, w_q.reshape(MLA_Q_LORA, N_HEADS * LANES).astype(BF16))
    kvd = _norm_matmul(p, (lat0 + MLA_Q_LORA) // MLA_KV_LORA, MLA_KV_LORA, kv_a_norm, w_kv_b.astype(BF16))
    out_d = _mla_attention(qd, kvd, p, (n_main // LANES), rope_tables, gq_mla, gk_mla, b, s)
    return out_c, out_d


def kernel(x, ffn_norm, ffn_w_gate_up, ffn_w_down, rel_bias, ev_norm, ev_w_in, ev_forget_bias, ev_fox_q_norm, ev_fox_k_norm, ev_w_out, od_norm, od_w_in, od_moba_q_norm, od_moba_k_norm, od_mla_q_a_norm, od_mla_w_q_b, od_mla_kv_a_norm, od_mla_w_kv_b, od_mla_q_norm, od_mla_k_norm, od_w_out):
    b, s, d = x.shape
    depth = ffn_norm.shape[0]
    bias_tiles = _bias_tiles(rel_bias)
    rope_tables = _rope_tables(s)
    x = x.reshape(b * s, d)
    for layer in range(depth):
        i = layer // 2
        if layer % 2 == 0:
            mixed = _even_mixer(x, b, s, ev_norm[i], ev_w_in[i], ev_forget_bias[i], ev_fox_q_norm[i],
                                ev_fox_k_norm[i])
            w_out = ev_w_out[i]
        else:
            mixed = _odd_mixer(x, b, s, od_norm[i], od_w_in[i], od_moba_q_norm[i], od_moba_k_norm[i],
                               od_mla_q_a_norm[i], od_mla_w_q_b[i], od_mla_kv_a_norm[i], od_mla_w_kv_b[i],
                               od_mla_q_norm[i], od_mla_k_norm[i], bias_tiles, rope_tables)
            w_out = od_w_out[i]
        x = _mix_ffn(mixed[0], mixed[1], w_out, x, ffn_norm[layer], ffn_w_gate_up[layer], ffn_w_down[layer])
    return x.reshape(b, s, d)
```

```python
import functools
import math

import jax
import jax.numpy as jnp
from jax import lax
from jax.experimental import pallas as pl
from jax.experimental.pallas import tpu as pltpu

F32 = jnp.float32
BF16 = jnp.bfloat16

HEAD_DIM = 64
N_HEADS = 8
N_PAIRS = N_HEADS // 2
LANES = 128
MOBA_BLOCK = 256
MOBA_TOPK = 3
MLA_Q_LORA = 256
MLA_KV_LORA = 128
MLA_NOPE = 64
MLA_ROPE = 32
MLA_QK = MLA_NOPE + MLA_ROPE
ROPE_BASE = 10000.0
REL_BUCKETS = 32
REL_MAX_EXACT = 16
REL_MAX_DISTANCE = 128
RMS_EPS = 1e-6
TQ = 256
KG = 512
SB_TQ = 256
TK = 256
VT_ROWS = 80
PREP_ROWS = 512
DEAD_LOG = -110.0
VMEM_LIMIT = 56 * 1024 * 1024
HIGHEST = lax.Precision.HIGHEST


def _cparams(*sem):
    return pltpu.CompilerParams(dimension_semantics=sem, vmem_limit_bytes=VMEM_LIMIT)


def _dot(a, b, **kw):
    return jnp.dot(a, b, preferred_element_type=F32, **kw)


def _log_sigmoid(z):
    return jnp.minimum(z, 0.0) - jnp.log1p(jnp.exp(-jnp.abs(z)))


def _lane_iota():
    return lax.broadcasted_iota(jnp.int32, (1, LANES), 1)


def _row_iota():
    return lax.broadcasted_iota(jnp.int32, (LANES, 1), 0)


def _pair_rms(x, gain):
    first = _lane_iota() < HEAD_DIM
    sq = x * x
    s0 = jnp.sum(jnp.where(first, sq, 0.0), axis=1, keepdims=True)
    s1 = jnp.sum(jnp.where(first, 0.0, sq), axis=1, keepdims=True)
    ms = jnp.where(first, s0, s1) * (1.0 / HEAD_DIM)
    return x * lax.rsqrt(ms + RMS_EPS) * gain


def _transposed(x):
    r = lax.broadcasted_iota(jnp.int32, (LANES, LANES), 0)
    c = lax.broadcasted_iota(jnp.int32, (LANES, LANES), 1)
    return lax.dot_general((r == c).astype(BF16), x, (((1,), (1,)), ((), ())), preferred_element_type=F32)


def _split3(c):
    hi = c.astype(BF16).astype(F32)
    mid = (c - hi).astype(BF16).astype(F32)
    return hi, mid, c - hi - mid


def _norm_matmul_kernel(x_ref, g_ref, w_ref, o_ref, *, n_chunk):
    x = x_ref[...]
    ms = jnp.mean(x * x, axis=-1, keepdims=True)
    h = (x * lax.rsqrt(ms + RMS_EPS) * g_ref[...]).astype(BF16)
    n = o_ref.shape[1]
    for c0 in range(0, n, n_chunk):
        c1 = min(n, c0 + n_chunk)
        o_ref[:, c0:c1] = _dot(h, w_ref[:, c0:c1])


def _norm_matmul(xw, col_blk, k, gain, w, tm=256, n_chunk=512):
    m = xw.shape[0]
    n = w.shape[1]
    return pl.pallas_call(
        functools.partial(_norm_matmul_kernel, n_chunk=n_chunk),
        grid=(m // tm,),
        in_specs=[
            pl.BlockSpec((tm, k), lambda i: (i, col_blk)),
            pl.BlockSpec((1, k), lambda i: (0, 0)),
            pl.BlockSpec((k, n), lambda i: (0, 0)),
        ],
        out_specs=pl.BlockSpec((tm, n), lambda i: (i, 0)),
        out_shape=jax.ShapeDtypeStruct((m, n), F32),
        compiler_params=_cparams("parallel"),
        name="norm_matmul",
    )(xw, gain.reshape(1, k), w)


def _mix_ffn_kernel(a_ref, b_ref, wa_ref, wb_ref, r_ref, g_ref, wg_ref, wu_ref, wd_ref, o_ref, x_s, h_s, acc_s):
    c = pl.program_id(1)

    @pl.when(c == 0)
    def _():
        x = r_ref[...] + (_dot(a_ref[...], wa_ref[...]) + _dot(b_ref[...], wb_ref[...]))
        x_s[...] = x
        ms = jnp.mean(x * x, axis=-1, keepdims=True)
        h_s[...] = (x * lax.rsqrt(ms + RMS_EPS) * g_ref[...]).astype(BF16)
        acc_s[...] = jnp.zeros_like(acc_s)

    h = h_s[...]
    g = _dot(h, wg_ref[...])
    u = _dot(h, wu_ref[...])
    act = (g * jax.nn.sigmoid(g) * u).astype(BF16)
    acc_s[...] += _dot(act, wd_ref[...])

    @pl.when(c == pl.num_programs(1) - 1)
    def _():
        o_ref[...] = x_s[...] + acc_s[...]


def _mix_ffn(a, b, w_out, res, gain, w_gate_up, w_down, tm=512, n_ff_chunks=2):
    m, d = res.shape
    ka = a.shape[1]
    d_ff = w_down.shape[0]
    tf = d_ff // n_ff_chunks
    assert tf * n_ff_chunks == d_ff and tf % LANES == 0
    wo = w_out.astype(BF16)
    wgu = w_gate_up.astype(BF16)
    wd = w_down.astype(BF16)
    return pl.pallas_call(
        _mix_ffn_kernel,
        grid=(m // tm, n_ff_chunks),
        in_specs=[
            pl.BlockSpec((tm, ka), lambda i, c: (i, 0)),
            pl.BlockSpec((tm, ka), lambda i, c: (i, 0)),
            pl.BlockSpec((ka, d), lambda i, c: (0, 0)),
            pl.BlockSpec((ka, d), lambda i, c: (1, 0)),
            pl.BlockSpec((tm, d), lambda i, c: (i, 0)),
            pl.BlockSpec((1, d), lambda i, c: (0, 0)),
            pl.BlockSpec((d, tf), lambda i, c: (0, c)),
            pl.BlockSpec((d, tf), lambda i, c: (0, n_ff_chunks + c)),
            pl.BlockSpec((tf, d), lambda i, c: (c, 0)),
        ],
        out_specs=pl.BlockSpec((tm, d), lambda i, c: (i, 0)),
        out_shape=jax.ShapeDtypeStruct((m, d), F32),
        scratch_shapes=[pltpu.VMEM((tm, d), F32), pltpu.VMEM((tm, d), BF16), pltpu.VMEM((tm, d), F32)],
        compiler_params=_cparams("parallel", "arbitrary"),
        name="mix_swiglu_ffn",
    )(a, b, wo, wo, res, gain.reshape(1, d), wgu, wgu, wd)


def _decay_kernel(f_ref, b_ref, c_ref, carry_s):
    @pl.when(pl.program_id(1) == 0)
    def _():
        carry_s[...] = jnp.zeros_like(carry_s)

    ts = f_ref.shape[0]
    logf = _log_sigmoid(f_ref[...] + b_ref[...])
    r = lax.broadcasted_iota(jnp.int32, (ts, ts), 0)
    c = lax.broadcasted_iota(jnp.int32, (ts, ts), 1)
    tri = (c <= r).astype(F32)
    cs = _dot(tri, logf, precision=HIGHEST) + carry_s[...]
    carry_s[...] = cs[ts - 1:ts, :]
    c_ref[...] = cs[:, 0:N_HEADS]


def _log_decay(p, col_blk, bias, b, s, ts=256):
    ns = s // ts
    bias128 = jnp.zeros((1, LANES), F32).at[0, :N_HEADS].set(bias)
    return pl.pallas_call(
        _decay_kernel,
        grid=(b, ns),
        in_specs=[
            pl.BlockSpec((ts, LANES), lambda bi, si: (bi * ns + si, col_blk)),
            pl.BlockSpec((1, LANES), lambda bi, si: (0, 0)),
        ],
        out_specs=pl.BlockSpec((ts, N_HEADS), lambda bi, si: (bi * ns + si, 0)),
        out_shape=jax.ShapeDtypeStruct((b * s, N_HEADS), F32),
        scratch_shapes=[pltpu.VMEM((1, LANES), F32)],
        compiler_params=_cparams("parallel", "arbitrary"),
        name="fox_log_decay",
    )(p, bias128)


def _attend(first_scores, first_group, scores_fn, vt_s, s_s, p_s, alive_fn=None):
    w = 2 * TQ
    s_s[0] = first_scores
    p_s[0] = jnp.zeros((KG, w), BF16)

    def values(g, pr, al, a0, a1):
        k0 = pl.multiple_of(g * KG, KG)
        a0 = al[:, 0:TQ] * a0 + _dot(vt_s[0, :, pl.ds(k0, KG)], pr[:, 0:TQ])
        a1 = al[:, TQ:w] * a1 + _dot(vt_s[1, :, pl.ds(k0, KG)], pr[:, TQ:w])
        return a0, a1

    def softmax(s, m):
        m_new = jnp.maximum(m, jnp.max(s, axis=0, keepdims=True))
        return m_new, jnp.exp(m - m_new), jnp.exp(s - m_new).astype(BF16)

    def cond(carry):
        return jnp.logical_and(carry[0] < first_group, carry[1] > 0)

    def body(carry):
        i, _, m, al, a0, a1 = carry
        par = lax.rem(i, 2)
        g = first_group - i
        pr_prev = p_s[par]
        s = s_s[par]
        a0, a1 = values(jnp.minimum(g + 1, first_group), pr_prev, al, a0, a1)
        s_next = scores_fn(g - 1)
        m_new, alpha, pr = softmax(s, m)
        s_s[1 - par] = s_next
        p_s[1 - par] = pr
        alive = jnp.int32(1) if alive_fn is None else alive_fn(g - 1, m_new).astype(jnp.int32)
        return i + 1, alive, m_new, alpha, a0, a1

    def cond2(carry):
        return jnp.logical_and(carry[0] + 1 < first_group, carry[1] > 0)

    def body2(carry):
        i, _, m, al, a0, a1 = carry
        par = lax.rem(i, 2)
        g = first_group - i
        pr_prev = p_s[par]
        s = s_s[par]
        a0, a1 = values(jnp.minimum(g + 1, first_group), pr_prev, al, a0, a1)
        s_1 = scores_fn(g - 1)
        m_0, alpha_0, pr_0 = softmax(s, m)
        a0, a1 = values(g, pr_0, alpha_0, a0, a1)
        s_2 = scores_fn(g - 2)
        m_1, alpha_1, pr_1 = softmax(s_1, m_0)
        s_s[par] = s_2
        p_s[par] = pr_1
        alive = jnp.int32(1) if alive_fn is None else alive_fn(g - 2, m_1).astype(jnp.int32)
        return i + 2, alive, m_1, alpha_1, a0, a1

    init = (jnp.int32(0), jnp.int32(1), jnp.full((1, w), -jnp.inf, F32), jnp.ones((1, w), F32),
            jnp.zeros((VT_ROWS, TQ), F32), jnp.zeros((VT_ROWS, TQ), F32))
    n, _, m, al, a0, a1 = lax.while_loop(cond, body, lax.while_loop(cond2, body2, init))
    par = lax.rem(n, 2)
    g = first_group - n
    a0, a1 = values(jnp.minimum(g + 1, first_group), p_s[par], al, a0, a1)
    _, alpha, pr = softmax(s_s[par], m)
    a0, a1 = values(g, pr, alpha, a0, a1)
    return (a0[0:HEAD_DIM, :] / a0[HEAD_DIM:HEAD_DIM + 1, :],
            a1[0:HEAD_DIM, :] / a1[HEAD_DIM:HEAD_DIM + 1, :])


def _score_scratch():
    return [pltpu.VMEM((2, KG, 2 * TQ), F32), pltpu.VMEM((2, KG, 2 * TQ), BF16)]


def _key_query_iotas(nk, nq):
    k = lax.broadcasted_iota(jnp.int32, (nk, nq), 0)
    q = lax.broadcasted_iota(jnp.int32, (nk, nq), 1)
    return k, q


def _pair_causal(nk, tq, shift=0, strict=False):
    kidx, qidx = _key_query_iotas(nk, 2 * tq)
    qidx = jnp.where(qidx >= tq, qidx - tq, qidx) + shift
    return kidx < qidx if strict else kidx <= qidx


def _values_with_ones(vt):
    return jnp.where(_row_iota() < HEAD_DIM, vt, 1.0)[0:VT_ROWS, :]


def _prep_loop(n_rows, fn):
    def body(i, carry):
        fn(pl.multiple_of(i * PREP_ROWS, PREP_ROWS))
        return carry

    lax.fori_loop(0, n_rows // PREP_ROWS, body, 0)


def _pair_specs(s, qcol, kcol, vcol):
    return [pl.BlockSpec((s, LANES), lambda b, hp, qi, col=col: (b, col + hp)) for col in (qcol, kcol, vcol)]


def _diag_group(qi):
    g0 = (qi * TQ + TQ - 1) // KG
    return g0, qi * TQ - g0 * KG


def _store_pair(o_ref, out0, out1):
    o_ref[...] = jnp.concatenate([out0, out1], axis=0).T.astype(o_ref.dtype)


def _fox_kernel(q_ref, k_ref, v_ref, c_ref, gq_ref, gk_ref, o_ref,
                qt_s, cq_s, ka_s, vt_s, s_s, p_s, *, seq):
    hp = pl.program_id(1)
    qi = pl.program_id(2)
    lane = _lane_iota()
    first = lane < HEAD_DIM
    head_lane = lax.broadcasted_iota(jnp.int32, (1, N_HEADS), 1)
    second = lax.broadcasted_iota(jnp.int32, (1, 2 * TQ), 1) >= TQ

    def augment(x, c, key_side):
        hi, mid, lo = _split3(c)
        sgn = -1.0 if key_side else 1.0
        c0 = HEAD_DIM + (3 if key_side else 0)
        one0 = HEAD_DIM + (0 if key_side else 3)
        out = jnp.where(first, x, 0.0)
        out = jnp.where(jnp.logical_and(lane >= one0, lane < one0 + 3), 1.0, out)
        for i, t in enumerate((hi, mid, lo)):
            out = jnp.where(lane == c0 + i, sgn * t, out)
        return out

    def head_col(cc, h):
        return jnp.sum(jnp.where(head_lane == h, cc, 0.0), axis=1, keepdims=True)

    @pl.when(qi == 0)
    def _():
        def prep(r0):
            rs = pl.ds(r0, PREP_ROWS)
            kn = _pair_rms(k_ref[rs, :], gk_ref[...])
            qn = _pair_rms(q_ref[rs, :], gq_ref[...]) * (HEAD_DIM ** -0.5)
            vt = v_ref[rs, :].T
            cc = c_ref[rs, :]
            for j in range(2):
                c = head_col(cc, 2 * hp + j)
                kj = kn if j == 0 else pltpu.roll(kn, HEAD_DIM, 1)
                ka_s[j, rs, :] = augment(kj, c, True).astype(BF16)
                vj = vt if j == 0 else pltpu.roll(vt, HEAD_DIM, 0)
                vt_s[j, :, rs] = _values_with_ones(vj).astype(BF16)
                qa = augment(qn if j == 0 else pltpu.roll(qn, HEAD_DIM, 1), c, False).T
                qt_s[j, :, rs] = qa.astype(BF16)
                cq_s[j, 0:1, rs] = jnp.sum(qa[HEAD_DIM:HEAD_DIM + 3, :], axis=0, keepdims=True)

        _prep_loop(seq, prep)

    qs = pl.ds(pl.multiple_of(qi * TQ, TQ), TQ)
    qt = [qt_s[0, :, qs], qt_s[1, :, qs]]
    qk_bound = 1.02 * HEAD_DIM ** 0.5 * (jnp.max(jnp.abs(gq_ref[...]), axis=1, keepdims=True)
                                          * jnp.max(jnp.abs(gk_ref[...]), axis=1, keepdims=True))
    score_bound = qk_bound + jnp.concatenate([cq_s[0, 0:1, qs], cq_s[1, 0:1, qs]], axis=1)

    def scores(g):
        ks = pl.ds(pl.multiple_of(g * KG, KG), KG)
        return jnp.concatenate([_dot(ka_s[0, ks, :], qt[0]), _dot(ka_s[1, ks, :], qt[1])], axis=1)

    def alive(g, m):
        row = c_ref[pl.ds((g + 1) * KG - 1, 1), :]
        c0 = jnp.sum(jnp.where(head_lane == 2 * hp, row, 0.0), axis=1, keepdims=True)
        c1 = jnp.sum(jnp.where(head_lane == 2 * hp + 1, row, 0.0), axis=1, keepdims=True)
        return jnp.max(score_bound - jnp.where(second, c1, c0) - m) > DEAD_LOG

    g0, shift = _diag_group(qi)
    diag = _pair_causal(KG, TQ, shift)
    _store_pair(o_ref, *_attend(jnp.where(diag, scores(g0), -jnp.inf), g0, scores, vt_s, s_s, p_s, alive))


def _fox_attention(p, cdecay, gq, gk, b, s):
    nq = s // TQ
    gq2 = jnp.tile(gq, 2).reshape(1, LANES)
    gk2 = jnp.tile(gk, 2).reshape(1, LANES)
    return pl.pallas_call(
        functools.partial(_fox_kernel, seq=s),
        grid=(b, N_PAIRS, nq),
        in_specs=_pair_specs(s, 0, N_PAIRS, 2 * N_PAIRS) + [
            pl.BlockSpec((s, N_HEADS), lambda b_, hp, qi: (b_, 0)),
            pl.BlockSpec((1, LANES), lambda b_, hp, qi: (0, 0)),
            pl.BlockSpec((1, LANES), lambda b_, hp, qi: (0, 0)),
        ],
        out_specs=pl.BlockSpec((TQ, LANES), lambda b_, hp, qi: (b_ * nq + qi, hp)),
        out_shape=jax.ShapeDtypeStruct((b * s, N_PAIRS * LANES), BF16),
        scratch_shapes=[pltpu.VMEM((2, LANES, s), BF16), pltpu.VMEM((2, 8, s), F32),
                        pltpu.VMEM((2, s, LANES), BF16), pltpu.VMEM((2, VT_ROWS, s), BF16)] + _score_scratch(),
        compiler_params=_cparams("parallel", "parallel", "arbitrary"),
        name="fox_attention",
    )(p, p, p, cdecay, gq2, gk2)


def _sb_kernel(q_ref, k_ref, v_ref, u_ref, o_ref, qt_s, kb_s, vt_s, *, seq):
    qi = pl.program_id(2)
    first = _lane_iota() < HEAD_DIM
    w = 2 * SB_TQ

    @pl.when(qi == 0)
    def _():
        def prep(r0):
            rs = pl.ds(r0, PREP_ROWS)
            kb_s[rs, :] = k_ref[rs, :].astype(BF16)
            vt_s[:, rs] = v_ref[rs, :].T.astype(BF16)
            q = q_ref[rs, :] * (HEAD_DIM ** -0.5)
            qt_s[0, :, rs] = jnp.where(first, q, 0.0).T.astype(BF16)
            qt_s[1, :, rs] = jnp.where(first, 0.0, q).T.astype(BF16)

        _prep_loop(seq, prep)

    qs = pl.ds(pl.multiple_of(qi * SB_TQ, SB_TQ), SB_TQ)
    qt = jnp.concatenate([qt_s[0, :, qs], qt_s[1, :, qs]], axis=1)

    def chunk(k0, nk, drop, acc, strict):
        z = _dot(kb_s[pl.ds(k0, nk), :], qt)
        softplus = jnp.maximum(z, 0.0) + jnp.log(1.0 + jnp.exp(-jnp.abs(z)))
        log_b = z - softplus
        if strict is not None:
            softplus = jnp.where(strict, softplus, 0.0)
        hi = softplus.astype(BF16)
        lo = (softplus - hi.astype(F32)).astype(BF16)
        later = u_ref[0:nk, 0:nk]
        wgt = jnp.exp(log_b - (drop + (_dot(later, hi) + _dot(later, lo))))
        if strict is not None:
            wgt = jnp.where(strict, wgt, 0.0)
        acc = acc + _dot(vt_s[:, pl.ds(k0, nk)], wgt.astype(BF16))
        return drop + jnp.sum(softplus, axis=0, keepdims=True), acc

    k0 = pl.multiple_of(jnp.maximum(qi - 1, 0) * TK, TK)
    strict = _pair_causal(2 * TK, SB_TQ, qi * SB_TQ - k0, strict=True)
    rsum, acc = chunk(k0, 2 * TK, jnp.zeros((1, w), F32), jnp.zeros((LANES, w), F32), strict)

    def cond(c):
        return jnp.logical_and(c[0] >= 0, c[1] > 0)

    def alive(drop):
        return (jnp.min(drop) < -DEAD_LOG).astype(jnp.int32)

    def body(c):
        rs, ac = chunk(pl.multiple_of(c[0] * TK, TK), TK, c[2], c[3], None)
        return c[0] - 1, alive(rs), rs, ac

    _, _, _, acc = lax.while_loop(cond, body, (qi - 2, alive(rsum), rsum, acc))
    _store_pair(o_ref, acc[0:HEAD_DIM, 0:SB_TQ], acc[HEAD_DIM:LANES, SB_TQ:w])


def _sb_attention(p, qcol, b, s):
    assert SB_TQ == TK
    nq = s // SB_TQ
    r = lax.broadcasted_iota(jnp.int32, (2 * TK, 2 * TK), 0)
    c = lax.broadcasted_iota(jnp.int32, (2 * TK, 2 * TK), 1)
    later = (c > r).astype(BF16)
    return pl.pallas_call(
        functools.partial(_sb_kernel, seq=s),
        grid=(b, N_PAIRS, nq),
        in_specs=_pair_specs(s, qcol, qcol + N_PAIRS, qcol + 2 * N_PAIRS) + [
            pl.BlockSpec((2 * TK, 2 * TK), lambda b_, hp, qi: (0, 0)),
        ],
        out_specs=pl.BlockSpec((SB_TQ, LANES), lambda b_, hp, qi: (b_ * nq + qi, hp)),
        out_shape=jax.ShapeDtypeStruct((b * s, N_PAIRS * LANES), BF16),
        scratch_shapes=[pltpu.VMEM((2, LANES, s), BF16), pltpu.VMEM((s, LANES), BF16), pltpu.VMEM((LANES, s), BF16)],
        compiler_params=_cparams("parallel", "parallel", "arbitrary"),
        name="stick_breaking_attention",
    )(p, p, p, later)


def _t5_bucket(rel):
    n = jnp.maximum(rel, 0)
    nf = jnp.maximum(n, 1).astype(F32)
    large = REL_MAX_EXACT + (jnp.log(nf / REL_MAX_EXACT) / math.log(REL_MAX_DISTANCE / REL_MAX_EXACT)
                             * (REL_BUCKETS - REL_MAX_EXACT)).astype(jnp.int32)
    large = jnp.minimum(large, REL_BUCKETS - 1)
    return jnp.where(n < REL_MAX_EXACT, n, large)


def _bias_tile_kernel(rb_ref, own_ref, prev_ref):
    hp = pl.program_id(0)
    kidx, qidx = _key_query_iotas(MOBA_BLOCK, MOBA_BLOCK)
    for off, ref in ((0, own_ref), (MOBA_BLOCK, prev_ref)):
        bucket = _t5_bucket(qidx - kidx + off)
        for j in range(2):
            out = jnp.zeros((MOBA_BLOCK, MOBA_BLOCK), F32)
            for bk in range(REL_BUCKETS):
                out = jnp.where(bucket == bk, rb_ref[2 * hp + j, bk], out)
            ref[0, :, j * MOBA_BLOCK:(j + 1) * MOBA_BLOCK] = out - rb_ref[2 * hp + j, REL_BUCKETS - 1]


def _bias_tiles(rel_bias):
    shape = jax.ShapeDtypeStruct((N_PAIRS, MOBA_BLOCK, 2 * MOBA_BLOCK), F32)
    spec = pl.BlockSpec((1, MOBA_BLOCK, 2 * MOBA_BLOCK), lambda h: (h, 0, 0))
    return pl.pallas_call(
        _bias_tile_kernel,
        grid=(N_PAIRS,),
        in_specs=[pl.BlockSpec(memory_space=pltpu.SMEM)],
        out_specs=[spec, spec],
        out_shape=[shape, shape],
        compiler_params=_cparams("arbitrary"),
        name="moba_bias_tiles",
    )(rel_bias)


def _moba_kernel(q_ref, k_ref, v_ref, bown_ref, bprev_ref, gq_ref, gk_ref, o_ref,
                 qt_s, kn_s, vt_s, km_s, sel_s, s_s, p_s, *, seq):
    qi = pl.program_id(2)
    nb = seq // MOBA_BLOCK
    first = _lane_iota() < HEAD_DIM
    w = 2 * TQ
    per_step = TQ // MOBA_BLOCK
    per_group = KG // MOBA_BLOCK

    @pl.when(qi == 0)
    def _():
        def prep_keys(n, carry):
            rs = pl.ds(pl.multiple_of(n * MOBA_BLOCK, MOBA_BLOCK), MOBA_BLOCK)
            kn = _pair_rms(k_ref[rs, :], gk_ref[...])
            kn_s[rs, :] = kn.astype(BF16)
            km_s[pl.ds(n, 1), :] = jnp.mean(kn, axis=0, keepdims=True)
            vt = _transposed(v_ref[rs, :].astype(BF16))
            vt_s[0, :, rs] = _values_with_ones(vt).astype(BF16)
            vt_s[1, :, rs] = _values_with_ones(pltpu.roll(vt, HEAD_DIM, 0)).astype(BF16)
            return carry

        lax.fori_loop(0, nb, prep_keys, 0)

        def prep_queries(m, carry):
            blk = lax.broadcasted_iota(jnp.int32, (nb, 1), 0)
            blocks = [2 * m, 2 * m + 1]
            rows = [pl.ds(pl.multiple_of(n * MOBA_BLOCK, MOBA_BLOCK), MOBA_BLOCK) for n in blocks]
            qhs = []
            for rs in rows:
                qn = _pair_rms(q_ref[rs, :], gq_ref[...]) * (HEAD_DIM ** -0.5)
                qhs.append((jnp.where(first, qn, 0.0), jnp.where(first, 0.0, qn)))
            nt = (((1,), (1,)), ((), ()))
            gates = [jnp.concatenate([lax.dot_general(km_s[...], qh, nt, precision=HIGHEST, preferred_element_type=F32)
                                      for qh in qh2], axis=1) for qh2 in qhs]
            for n, rs, qh2, gate in zip(blocks, rows, qhs, gates):
                past = blk < n
                gate = jnp.where(past, gate, -jnp.inf)
                beaten = jnp.zeros((nb, 2 * MOBA_BLOCK), F32)
                for n2 in range(nb):
                    g2 = gate[n2:n2 + 1, :]
                    wins = jnp.logical_or(g2 > gate, jnp.logical_and(g2 == gate, n2 < blk))
                    beaten = beaten + jnp.where(wins, 1.0, 0.0)
                sel = jnp.where(jnp.logical_and(beaten < min(MOBA_TOPK, nb), past), 1.0, 0.0)
                for j in range(2):
                    qt_s[j, :, rs] = _transposed(qh2[j].astype(BF16)).astype(BF16)
                    sel_s[j, :, rs] = sel[:, j * MOBA_BLOCK:(j + 1) * MOBA_BLOCK]
            return carry

        lax.fori_loop(0, nb // 2, prep_queries, 0)

    qs = pl.ds(pl.multiple_of(qi * TQ, TQ), TQ)
    qt = jnp.concatenate([qt_s[0, :, qs], qt_s[1, :, qs]], axis=1)
    lane_w = lax.broadcasted_iota(jnp.int32, (1, w), 1)
    in_step = jnp.where(lane_w >= TQ, lane_w - TQ, lane_w)
    cur = qi * per_step + jnp.right_shift(in_step, MOBA_BLOCK.bit_length() - 1)

    def per_query_block(ref):
        return jnp.concatenate([ref[0, :, j * MOBA_BLOCK:(j + 1) * MOBA_BLOCK]
                                for j in range(2) for _ in range(per_step)], axis=1)

    def group_scores(g, diagonal):
        raw = _dot(kn_s[pl.ds(pl.multiple_of(g * KG, KG), KG), :], qt)
        pieces = []
        for i in range(per_group):
            n = g * per_group + i
            s = raw[i * MOBA_BLOCK:(i + 1) * MOBA_BLOCK, :]
            mask = jnp.concatenate([sel_s[0, pl.ds(n, 1), qs], sel_s[1, pl.ds(n, 1), qs]], axis=1) > 0.5
            follows = cur == n + 1
            if diagonal:
                own = cur == n
                kidx, qidx = _key_query_iotas(MOBA_BLOCK, w)
                causal = kidx <= jnp.bitwise_and(qidx, MOBA_BLOCK - 1)
                mask = jnp.logical_or(jnp.logical_and(own, causal), jnp.logical_and(jnp.logical_not(own), mask))
                s = s + jnp.where(own, per_query_block(bown_ref), jnp.where(follows, per_query_block(bprev_ref), 0.0))
            elif i == per_group - 1:
                s = s + jnp.where(follows, per_query_block(bprev_ref), 0.0)
            pieces.append(jnp.where(mask, s, -jnp.inf))
        return jnp.concatenate(pieces, axis=0)

    g0, _ = _diag_group(qi)
    _store_pair(o_ref, *_attend(group_scores(g0, True), g0, lambda g: group_scores(g, False), vt_s, s_s, p_s))


def _moba_attention(p, bown, bprev, gq, gk, b, s):
    assert KG % TQ == 0 and TQ % MOBA_BLOCK == 0 and s % KG == 0
    nq = s // TQ
    nb = s // MOBA_BLOCK
    gq2 = jnp.tile(gq, 2).reshape(1, LANES)
    gk2 = jnp.tile(gk, 2).reshape(1, LANES)
    bias_spec = pl.BlockSpec((1, MOBA_BLOCK, 2 * MOBA_BLOCK), lambda b_, hp, qi: (hp, 0, 0))
    return pl.pallas_call(
        functools.partial(_moba_kernel, seq=s),
        grid=(b, N_PAIRS, nq),
        in_specs=_pair_specs(s, 0, N_PAIRS, 2 * N_PAIRS) + [
            bias_spec, bias_spec,
            pl.BlockSpec((1, LANES), lambda b_, hp, qi: (0, 0)),
            pl.BlockSpec((1, LANES), lambda b_, hp, qi: (0, 0)),
        ],
        out_specs=pl.BlockSpec((TQ, LANES), lambda b_, hp, qi: (b_ * nq + qi, hp)),
        out_shape=jax.ShapeDtypeStruct((b * s, N_PAIRS * LANES), BF16),
        scratch_shapes=[pltpu.VMEM((2, LANES, s), BF16), pltpu.VMEM((s, LANES), BF16),
                        pltpu.VMEM((2, VT_ROWS, s), BF16), pltpu.VMEM((nb, LANES), F32),
                        pltpu.VMEM((2, nb, s), F32)] + _score_scratch(),
        compiler_params=_cparams("parallel", "parallel", "arbitrary"),
        name="moba_attention",
    )(p, p, p, bown, bprev, gq2, gk2)


def _rope(x, cos, sin):
    return x * cos + pltpu.roll(x, LANES - MLA_ROPE // 2, 1) * sin


def _mla_norm_rope(x, gain, cos, sin):
    ms = jnp.sum(jnp.where(_lane_iota() < MLA_QK, x * x, 0.0), axis=1, keepdims=True) * (1.0 / MLA_QK)
    return _rope(x * lax.rsqrt(ms + RMS_EPS) * gain, cos, sin)


def _mla_lanes(t):
    x1 = t[..., MLA_NOPE:MLA_NOPE + MLA_ROPE // 2]
    pad = jnp.zeros(t.shape[:-1] + (LANES - MLA_QK - MLA_ROPE // 2,), t.dtype)
    return jnp.concatenate([t, x1, pad], axis=-1)


def _mla_kernel(q_ref, kv_ref, kr_ref, cos_ref, sin_ref, gq_ref, gk_ref, o_ref, qt_s, kn_s, vt_s, s_s, p_s, *, seq):
    qi = pl.program_id(2)
    first = _lane_iota() < HEAD_DIM

    @pl.when(qi == 0)
    def _():
        def prep(r0):
            rs = pl.ds(r0, PREP_ROWS)
            cos, sin = cos_ref[rs, :], sin_ref[rs, :]
            for j in range(2):
                kv = kv_ref[rs, j * LANES:(j + 1) * LANES]
                k = jnp.where(first, kv, kr_ref[rs, :])
                kn_s[j, rs, :] = _mla_norm_rope(k, gk_ref[...], cos, sin).astype(BF16)
                vt_s[j, :, rs] = _values_with_ones(pltpu.roll(_transposed(kv.astype(BF16)), HEAD_DIM, 0)).astype(BF16)
                q = q_ref[rs, j * LANES:(j + 1) * LANES]
                qt_s[j, :, rs] = _transposed(_mla_norm_rope(q, gq_ref[...], cos, sin).astype(BF16)).astype(BF16)

        _prep_loop(seq, prep)

    qs = pl.ds(pl.multiple_of(qi * TQ, TQ), TQ)
    qt = [qt_s[0, :, qs], qt_s[1, :, qs]]
    scale = MLA_QK ** -0.5

    def scores(g):
        ks = pl.ds(pl.multiple_of(g * KG, KG), KG)
        return jnp.concatenate([_dot(kn_s[0, ks, :], qt[0]), _dot(kn_s[1, ks, :], qt[1])], axis=1) * scale

    g0, shift = _diag_group(qi)
    diag = _pair_causal(KG, TQ, shift)
    _store_pair(o_ref, *_attend(jnp.where(diag, scores(g0), -jnp.inf), g0, scores, vt_s, s_s, p_s))


def _mla_attention(qd, kvd, p, kr_col, tables, gq, gk, b, s):
    nq = s // TQ
    cos, sin = tables
    gq128 = _mla_lanes(gq).reshape(1, LANES)
    gk128 = _mla_lanes(gk).reshape(1, LANES)
    full = pl.BlockSpec((s, LANES), lambda b_, hp, qi: (0, 0))
    gain = pl.BlockSpec((1, LANES), lambda b_, hp, qi: (0, 0))
    return pl.pallas_call(
        functools.partial(_mla_kernel, seq=s),
        grid=(b, N_PAIRS, nq),
        in_specs=[
            pl.BlockSpec((s, 2 * LANES), lambda b_, hp, qi: (b_, hp)),
            pl.BlockSpec((s, 2 * LANES), lambda b_, hp, qi: (b_, hp)),
            pl.BlockSpec((s, LANES), lambda b_, hp, qi: (b_, kr_col)),
            full, full, gain, gain,
        ],
        out_specs=pl.BlockSpec((TQ, LANES), lambda b_, hp, qi: (b_ * nq + qi, hp)),
        out_shape=jax.ShapeDtypeStruct((b * s, N_PAIRS * LANES), BF16),
        scratch_shapes=[pltpu.VMEM((2, LANES, s), BF16), pltpu.VMEM((2, s, LANES), BF16),
                        pltpu.VMEM((2, VT_ROWS, s), BF16)] + _score_scratch(),
        compiler_params=_cparams("parallel", "parallel", "arbitrary"),
        name="mla_attention",
    )(qd, kvd, p, cos, sin, gq128, gk128)


def _rope_tables(s):
    half = MLA_ROPE // 2
    inv_freq = ROPE_BASE ** (-jnp.arange(half, dtype=F32) / half)
    ang = jnp.arange(s, dtype=F32)[:, None] * inv_freq[None, :]
    cos, sin = jnp.cos(ang), jnp.sin(ang)
    z = lambda w: jnp.zeros((s, w), F32)
    cos_t = jnp.concatenate([jnp.ones((s, MLA_NOPE), F32), cos, cos, z(LANES - MLA_QK)], axis=1)
    sin_t = jnp.concatenate([z(MLA_NOPE), -sin, sin, z(LANES - MLA_QK)], axis=1)
    return cos_t, sin_t


def _even_mixer(x, b, s, norm, w_in, forget_bias, gq, gk):
    d = x.shape[1]
    w = HEAD_DIM * N_HEADS
    cuts = [0, w, 2 * w, 3 * w, 3 * w + N_HEADS, 4 * w + N_HEADS, 5 * w + N_HEADS, 6 * w + N_HEADS]
    qa, ka, va, fa, qb, kb, vb = (w_in[:, cuts[i]:cuts[i + 1]] for i in range(7))
    w_perm = jnp.concatenate([qa, ka, va, qb, kb, vb, fa, jnp.zeros((d, LANES - N_HEADS), F32)], axis=1).astype(BF16)
    p = _norm_matmul(x, 0, d, norm, w_perm)
    cdecay = _log_decay(p, 6 * N_PAIRS, forget_bias, b, s)
    out_a = _fox_attention(p, cdecay, gq, gk, b, s)
    out_b = _sb_attention(p, 3 * N_PAIRS, b, s)
    return out_a, out_b


def _odd_mixer(x, b, s, norm, w_in, gq_moba, gk_moba, q_a_norm, w_q_b, kv_a_norm, w_kv_b, gq_mla, gk_mla,
               bias_tiles, rope_tables):
    d = x.shape[1]
    n_main = 3 * HEAD_DIM * N_HEADS + MLA_Q_LORA + MLA_KV_LORA
    w_rope = _mla_lanes(jnp.concatenate([jnp.zeros((d, MLA_NOPE), F32), w_in[:, n_main:]], axis=1))
    w_perm = jnp.concatenate([w_in[:, :n_main], w_rope], axis=1).astype(BF16)
    p = _norm_matmul(x, 0, d, norm, w_perm)
    out_c = _moba_attention(p, bias_tiles[0], bias_tiles[1], gq_moba, gk_moba, b, s)
    lat0 = 3 * HEAD_DIM * N_HEADS
    w_q = _mla_lanes(w_q_b.reshape(MLA_Q_LORA, N_HEADS, MLA_QK))
    qd = _norm_matmul(p, lat0 // MLA_Q_LORA, MLA_Q_LORA, q_a_norm, w_q.reshape(MLA_Q_LORA, N_HEADS * LANES).astype(BF16))
    kvd = _norm_matmul(p, (lat0 + MLA_Q_LORA) // MLA_KV_LORA, MLA_KV_LORA, kv_a_norm, w_kv_b.astype(BF16))
    out_d = _mla_attention(qd, kvd, p, (n_main // LANES), rope_tables, gq_mla, gk_mla, b, s)
    return out_c, out_d


def kernel(x, ffn_norm, ffn_w_gate_up, ffn_w_down, rel_bias, ev_norm, ev_w_in, ev_forget_bias, ev_fox_q_norm, ev_fox_k_norm, ev_w_out, od_norm, od_w_in, od_moba_q_norm, od_moba_k_norm, od_mla_q_a_norm, od_mla_w_q_b, od_mla_kv_a_norm, od_mla_w_kv_b, od_mla_q_norm, od_mla_k_norm, od_w_out):
    b, s, d = x.shape
    depth = ffn_norm.shape[0]
    bias_tiles = _bias_tiles(rel_bias)
    rope_tables = _rope_tables(s)
    x = x.reshape(b * s, d)
    for layer in range(depth):
        i = layer // 2
        if layer % 2 == 0:
            mixed = _even_mixer(x, b, s, ev_norm[i], ev_w_in[i], ev_forget_bias[i], ev_fox_q_norm[i],
                                ev_fox_k_norm[i])
            w_out = ev_w_out[i]
        else:
            mixed = _odd_mixer(x, b, s, od_norm[i], od_w_in[i], od_moba_q_norm[i], od_moba_k_norm[i],
                               od_mla_q_a_norm[i], od_mla_w_q_b[i], od_mla_kv_a_norm[i], od_mla_w_kv_b[i],
                               od_mla_q_norm[i], od_mla_k_norm[i], bias_tiles, rope_tables)
            w_out = od_w_out[i]
        x = _mix_ffn(mixed[0], mixed[1], w_out, x, ffn_norm[layer], ffn_w_gate_up[layer], ffn_w_down[layer])
    return x.reshape(b, s, d)
```

```python
import functools
import math

import jax
import jax.numpy as jnp
from jax import lax
from jax.experimental import pallas as pl
from jax.experimental.pallas import tpu as pltpu

F32 = jnp.float32
BF16 = jnp.bfloat16

HEAD_DIM = 64
N_HEADS = 8
N_PAIRS = N_HEADS // 2
LANES = 128
MOBA_BLOCK = 256
MOBA_TOPK = 3
MLA_Q_LORA = 256
MLA_KV_LORA = 128
MLA_NOPE = 64
MLA_ROPE = 32
MLA_QK = MLA_NOPE + MLA_ROPE
ROPE_BASE = 10000.0
REL_BUCKETS = 32
REL_MAX_EXACT = 16
REL_MAX_DISTANCE = 128
RMS_EPS = 1e-6
TQ = 256
KG = 512
SB_TQ = 256
TK = 256
VT_ROWS = 80
PREP_ROWS = 512
DEAD_LOG = -110.0
VMEM_LIMIT = 56 * 1024 * 1024
ROW_TILE_BYTES = 12 * 1024 * 1024
MAX_ROW_TILE = 1024
HIGHEST = lax.Precision.HIGHEST


def _cparams(*sem):
    return pltpu.CompilerParams(dimension_semantics=sem, vmem_limit_bytes=VMEM_LIMIT)


def _dot(a, b, **kw):
    return jnp.dot(a, b, preferred_element_type=F32, **kw)


def _log_sigmoid(z):
    return jnp.minimum(z, 0.0) - jnp.log1p(jnp.exp(-jnp.abs(z)))


def _lane_iota():
    return lax.broadcasted_iota(jnp.int32, (1, LANES), 1)


def _row_iota():
    return lax.broadcasted_iota(jnp.int32, (LANES, 1), 0)


def _pair_rms(x, gain):
    first = _lane_iota() < HEAD_DIM
    sq = x * x
    s0 = jnp.sum(jnp.where(first, sq, 0.0), axis=1, keepdims=True)
    s1 = jnp.sum(jnp.where(first, 0.0, sq), axis=1, keepdims=True)
    ms = jnp.where(first, s0, s1) * (1.0 / HEAD_DIM)
    return x * lax.rsqrt(ms + RMS_EPS) * gain


def _transposed(x):
    r = lax.broadcasted_iota(jnp.int32, (LANES, LANES), 0)
    c = lax.broadcasted_iota(jnp.int32, (LANES, LANES), 1)
    return lax.dot_general((r == c).astype(BF16), x, (((1,), (1,)), ((), ())), preferred_element_type=F32)


def _split3(c):
    hi = c.astype(BF16).astype(F32)
    mid = (c - hi).astype(BF16).astype(F32)
    return hi, mid, c - hi - mid


def _norm_matmul_kernel(x_ref, g_ref, w_ref, o_ref, *, n_chunk):
    x = x_ref[...]
    ms = jnp.mean(x * x, axis=-1, keepdims=True)
    h = (x * lax.rsqrt(ms + RMS_EPS) * g_ref[...]).astype(BF16)
    n = o_ref.shape[1]
    for c0 in range(0, n, n_chunk):
        c1 = min(n, c0 + n_chunk)
        o_ref[:, c0:c1] = _dot(h, w_ref[:, c0:c1])


def _norm_matmul(xw, col_blk, k, gain, w, n_chunk=512):
    m = xw.shape[0]
    n = w.shape[1]
    tm = min(m, MAX_ROW_TILE, 1 << ((ROW_TILE_BYTES // (4 * (k + n))).bit_length() - 1))
    assert m % tm == 0
    return pl.pallas_call(
        functools.partial(_norm_matmul_kernel, n_chunk=n_chunk),
        grid=(m // tm,),
        in_specs=[
            pl.BlockSpec((tm, k), lambda i: (i, col_blk)),
            pl.BlockSpec((1, k), lambda i: (0, 0)),
            pl.BlockSpec((k, n), lambda i: (0, 0)),
        ],
        out_specs=pl.BlockSpec((tm, n), lambda i: (i, 0)),
        out_shape=jax.ShapeDtypeStruct((m, n), F32),
        compiler_params=_cparams("parallel"),
        name="norm_matmul",
    )(xw, gain.reshape(1, k), w)


def _mix_ffn_kernel(a_ref, b_ref, wa_ref, wb_ref, r_ref, g_ref, wg_ref, wu_ref, wd_ref, o_ref, x_s, h_s, acc_s):
    c = pl.program_id(1)

    @pl.when(c == 0)
    def _():
        x = r_ref[...] + (_dot(a_ref[...], wa_ref[...]) + _dot(b_ref[...], wb_ref[...]))
        x_s[...] = x
        ms = jnp.mean(x * x, axis=-1, keepdims=True)
        h_s[...] = (x * lax.rsqrt(ms + RMS_EPS) * g_ref[...]).astype(BF16)
        acc_s[...] = jnp.zeros_like(acc_s)

    h = h_s[...]
    g = _dot(h, wg_ref[...])
    u = _dot(h, wu_ref[...])
    act = (g * jax.nn.sigmoid(g) * u).astype(BF16)
    acc_s[...] += _dot(act, wd_ref[...])

    @pl.when(c == pl.num_programs(1) - 1)
    def _():
        o_ref[...] = x_s[...] + acc_s[...]


def _mix_ffn(a, b, w_out, res, gain, w_gate_up, w_down, tm=512, n_ff_chunks=2):
    m, d = res.shape
    ka = a.shape[1]
    d_ff = w_down.shape[0]
    tf = d_ff // n_ff_chunks
    assert tf * n_ff_chunks == d_ff and tf % LANES == 0
    wo = w_out.astype(BF16)
    wgu = w_gate_up.astype(BF16)
    wd = w_down.astype(BF16)
    return pl.pallas_call(
        _mix_ffn_kernel,
        grid=(m // tm, n_ff_chunks),
        in_specs=[
            pl.BlockSpec((tm, ka), lambda i, c: (i, 0)),
            pl.BlockSpec((tm, ka), lambda i, c: (i, 0)),
            pl.BlockSpec((ka, d), lambda i, c: (0, 0)),
            pl.BlockSpec((ka, d), lambda i, c: (1, 0)),
            pl.BlockSpec((tm, d), lambda i, c: (i, 0)),
            pl.BlockSpec((1, d), lambda i, c: (0, 0)),
            pl.BlockSpec((d, tf), lambda i, c: (0, c)),
            pl.BlockSpec((d, tf), lambda i, c: (0, n_ff_chunks + c)),
            pl.BlockSpec((tf, d), lambda i, c: (c, 0)),
        ],
        out_specs=pl.BlockSpec((tm, d), lambda i, c: (i, 0)),
        out_shape=jax.ShapeDtypeStruct((m, d), F32),
        scratch_shapes=[pltpu.VMEM((tm, d), F32), pltpu.VMEM((tm, d), BF16), pltpu.VMEM((tm, d), F32)],
        compiler_params=_cparams("parallel", "arbitrary"),
        name="mix_swiglu_ffn",
    )(a, b, wo, wo, res, gain.reshape(1, d), wgu, wgu, wd)


def _decay_kernel(f_ref, b_ref, c_ref, carry_s):
    @pl.when(pl.program_id(1) == 0)
    def _():
        carry_s[...] = jnp.zeros_like(carry_s)

    ts = f_ref.shape[0]
    logf = _log_sigmoid(f_ref[...] + b_ref[...])
    r = lax.broadcasted_iota(jnp.int32, (ts, ts), 0)
    c = lax.broadcasted_iota(jnp.int32, (ts, ts), 1)
    tri = (c <= r).astype(F32)
    cs = _dot(tri, logf, precision=HIGHEST) + carry_s[...]
    carry_s[...] = cs[ts - 1:ts, :]
    c_ref[...] = cs[:, 0:N_HEADS]


def _log_decay(p, col_blk, bias, b, s, ts=256):
    ns = s // ts
    bias128 = jnp.zeros((1, LANES), F32).at[0, :N_HEADS].set(bias)
    return pl.pallas_call(
        _decay_kernel,
        grid=(b, ns),
        in_specs=[
            pl.BlockSpec((ts, LANES), lambda bi, si: (bi * ns + si, col_blk)),
            pl.BlockSpec((1, LANES), lambda bi, si: (0, 0)),
        ],
        out_specs=pl.BlockSpec((ts, N_HEADS), lambda bi, si: (bi * ns + si, 0)),
        out_shape=jax.ShapeDtypeStruct((b * s, N_HEADS), F32),
        scratch_shapes=[pltpu.VMEM((1, LANES), F32)],
        compiler_params=_cparams("parallel", "arbitrary"),
        name="fox_log_decay",
    )(p, bias128)


def _attend(first_scores, first_group, scores_fn, vt_s, s_s, p_s, alive_fn=None):
    w = 2 * TQ
    s_s[0] = first_scores
    p_s[0] = jnp.zeros((KG, w), BF16)

    def values(g, pr, al, a0, a1):
        k0 = pl.multiple_of(g * KG, KG)
        a0 = al[:, 0:TQ] * a0 + _dot(vt_s[0, :, pl.ds(k0, KG)], pr[:, 0:TQ])
        a1 = al[:, TQ:w] * a1 + _dot(vt_s[1, :, pl.ds(k0, KG)], pr[:, TQ:w])
        return a0, a1

    def softmax(s, m):
        m_new = jnp.maximum(m, jnp.max(s, axis=0, keepdims=True))
        return m_new, jnp.exp(m - m_new), jnp.exp(s - m_new).astype(BF16)

    def cond(carry):
        return jnp.logical_and(carry[0] < first_group, carry[1] > 0)

    def body(carry):
        i, _, m, al, a0, a1 = carry
        par = lax.rem(i, 2)
        g = first_group - i
        pr_prev = p_s[par]
        s = s_s[par]
        a0, a1 = values(jnp.minimum(g + 1, first_group), pr_prev, al, a0, a1)
        s_next = scores_fn(g - 1)
        m_new, alpha, pr = softmax(s, m)
        s_s[1 - par] = s_next
        p_s[1 - par] = pr
        alive = jnp.int32(1) if alive_fn is None else alive_fn(g - 1, m_new).astype(jnp.int32)
        return i + 1, alive, m_new, alpha, a0, a1

    def cond2(carry):
        return jnp.logical_and(carry[0] + 1 < first_group, carry[1] > 0)

    def body2(carry):
        i, _, m, al, a0, a1 = carry
        par = lax.rem(i, 2)
        g = first_group - i
        pr_prev = p_s[par]
        s = s_s[par]
        a0, a1 = values(jnp.minimum(g + 1, first_group), pr_prev, al, a0, a1)
        s_1 = scores_fn(g - 1)
        m_0, alpha_0, pr_0 = softmax(s, m)
        a0, a1 = values(g, pr_0, alpha_0, a0, a1)
        s_2 = scores_fn(g - 2)
        m_1, alpha_1, pr_1 = softmax(s_1, m_0)
        s_s[par] = s_2
        p_s[par] = pr_1
        alive = jnp.int32(1) if alive_fn is None else alive_fn(g - 2, m_1).astype(jnp.int32)
        return i + 2, alive, m_1, alpha_1, a0, a1

    init = (jnp.int32(0), jnp.int32(1), jnp.full((1, w), -jnp.inf, F32), jnp.ones((1, w), F32),
            jnp.zeros((VT_ROWS, TQ), F32), jnp.zeros((VT_ROWS, TQ), F32))
    n, _, m, al, a0, a1 = lax.while_loop(cond, body, lax.while_loop(cond2, body2, init))
    par = lax.rem(n, 2)
    g = first_group - n
    a0, a1 = values(jnp.minimum(g + 1, first_group), p_s[par], al, a0, a1)
    _, alpha, pr = softmax(s_s[par], m)
    a0, a1 = values(g, pr, alpha, a0, a1)
    return (a0[0:HEAD_DIM, :] / a0[HEAD_DIM:HEAD_DIM + 1, :],
            a1[0:HEAD_DIM, :] / a1[HEAD_DIM:HEAD_DIM + 1, :])


def _score_scratch():
    return [pltpu.VMEM((2, KG, 2 * TQ), F32), pltpu.VMEM((2, KG, 2 * TQ), BF16)]


def _key_query_iotas(nk, nq):
    k = lax.broadcasted_iota(jnp.int32, (nk, nq), 0)
    q = lax.broadcasted_iota(jnp.int32, (nk, nq), 1)
    return k, q


def _pair_causal(nk, tq, shift=0, strict=False):
    kidx, qidx = _key_query_iotas(nk, 2 * tq)
    qidx = jnp.where(qidx >= tq, qidx - tq, qidx) + shift
    return kidx < qidx if strict else kidx <= qidx


def _values_with_ones(vt):
    return jnp.where(_row_iota() < HEAD_DIM, vt, 1.0)[0:VT_ROWS, :]


def _prep_loop(n_rows, fn):
    def body(i, carry):
        fn(pl.multiple_of(i * PREP_ROWS, PREP_ROWS))
        return carry

    lax.fori_loop(0, n_rows // PREP_ROWS, body, 0)


def _pair_specs(s, qcol, kcol, vcol):
    return [pl.BlockSpec((s, LANES), lambda b, hp, qi, col=col: (b, col + hp)) for col in (qcol, kcol, vcol)]


def _diag_group(qi):
    g0 = (qi * TQ + TQ - 1) // KG
    return g0, qi * TQ - g0 * KG


def _store_pair(o_ref, out0, out1):
    o_ref[...] = jnp.concatenate([out0, out1], axis=0).T.astype(o_ref.dtype)


def _fox_kernel(q_ref, k_ref, v_ref, c_ref, gq_ref, gk_ref, o_ref,
                qt_s, cq_s, ka_s, vt_s, s_s, p_s, *, seq):
    hp = pl.program_id(1)
    qi = pl.program_id(2)
    lane = _lane_iota()
    first = lane < HEAD_DIM
    head_lane = lax.broadcasted_iota(jnp.int32, (1, N_HEADS), 1)
    second = lax.broadcasted_iota(jnp.int32, (1, 2 * TQ), 1) >= TQ

    def augment(x, c, key_side):
        hi, mid, lo = _split3(c)
        sgn = -1.0 if key_side else 1.0
        c0 = HEAD_DIM + (3 if key_side else 0)
        one0 = HEAD_DIM + (0 if key_side else 3)
        out = jnp.where(first, x, 0.0)
        out = jnp.where(jnp.logical_and(lane >= one0, lane < one0 + 3), 1.0, out)
        for i, t in enumerate((hi, mid, lo)):
            out = jnp.where(lane == c0 + i, sgn * t, out)
        return out

    def head_col(cc, h):
        return jnp.sum(jnp.where(head_lane == h, cc, 0.0), axis=1, keepdims=True)

    @pl.when(qi == 0)
    def _():
        def prep(r0):
            rs = pl.ds(r0, PREP_ROWS)
            kn = _pair_rms(k_ref[rs, :], gk_ref[...])
            qn = _pair_rms(q_ref[rs, :], gq_ref[...]) * (HEAD_DIM ** -0.5)
            vt = v_ref[rs, :].T
            cc = c_ref[rs, :]
            for j in range(2):
                c = head_col(cc, 2 * hp + j)
                kj = kn if j == 0 else pltpu.roll(kn, HEAD_DIM, 1)
                ka_s[j, rs, :] = augment(kj, c, True).astype(BF16)
                vj = vt if j == 0 else pltpu.roll(vt, HEAD_DIM, 0)
                vt_s[j, :, rs] = _values_with_ones(vj).astype(BF16)
                qa = augment(qn if j == 0 else pltpu.roll(qn, HEAD_DIM, 1), c, False).T
                qt_s[j, :, rs] = qa.astype(BF16)
                cq_s[j, 0:1, rs] = jnp.sum(qa[HEAD_DIM:HEAD_DIM + 3, :], axis=0, keepdims=True)

        _prep_loop(seq, prep)

    qs = pl.ds(pl.multiple_of(qi * TQ, TQ), TQ)
    qt = [qt_s[0, :, qs], qt_s[1, :, qs]]
    qk_bound = 1.02 * HEAD_DIM ** 0.5 * (jnp.max(jnp.abs(gq_ref[...]), axis=1, keepdims=True)
                                          * jnp.max(jnp.abs(gk_ref[...]), axis=1, keepdims=True))
    score_bound = qk_bound + jnp.concatenate([cq_s[0, 0:1, qs], cq_s[1, 0:1, qs]], axis=1)

    def scores(g):
        ks = pl.ds(pl.multiple_of(g * KG, KG), KG)
        return jnp.concatenate([_dot(ka_s[0, ks, :], qt[0]), _dot(ka_s[1, ks, :], qt[1])], axis=1)

    def alive(g, m):
        row = c_ref[pl.ds((g + 1) * KG - 1, 1), :]
        c0 = jnp.sum(jnp.where(head_lane == 2 * hp, row, 0.0), axis=1, keepdims=True)
        c1 = jnp.sum(jnp.where(head_lane == 2 * hp + 1, row, 0.0), axis=1, keepdims=True)
        return jnp.max(score_bound - jnp.where(second, c1, c0) - m) > DEAD_LOG

    g0, shift = _diag_group(qi)
    diag = _pair_causal(KG, TQ, shift)
    _store_pair(o_ref, *_attend(jnp.where(diag, scores(g0), -jnp.inf), g0, scores, vt_s, s_s, p_s, alive))


def _fox_attention(p, cdecay, gq, gk, b, s):
    nq = s // TQ
    gq2 = jnp.tile(gq, 2).reshape(1, LANES)
    gk2 = jnp.tile(gk, 2).reshape(1, LANES)
    return pl.pallas_call(
        functools.partial(_fox_kernel, seq=s),
        grid=(b, N_PAIRS, nq),
        in_specs=_pair_specs(s, 0, N_PAIRS, 2 * N_PAIRS) + [
            pl.BlockSpec((s, N_HEADS), lambda b_, hp, qi: (b_, 0)),
            pl.BlockSpec((1, LANES), lambda b_, hp, qi: (0, 0)),
            pl.BlockSpec((1, LANES), lambda b_, hp, qi: (0, 0)),
        ],
        out_specs=pl.BlockSpec((TQ, LANES), lambda b_, hp, qi: (b_ * nq + qi, hp)),
        out_shape=jax.ShapeDtypeStruct((b * s, N_PAIRS * LANES), BF16),
        scratch_shapes=[pltpu.VMEM((2, LANES, s), BF16), pltpu.VMEM((2, 8, s), F32),
                        pltpu.VMEM((2, s, LANES), BF16), pltpu.VMEM((2, VT_ROWS, s), BF16)] + _score_scratch(),
        compiler_params=_cparams("parallel", "parallel", "arbitrary"),
        name="fox_attention",
    )(p, p, p, cdecay, gq2, gk2)


def _sb_kernel(q_ref, k_ref, v_ref, u_ref, o_ref, qt_s, kb_s, vt_s, *, seq):
    qi = pl.program_id(2)
    first = _lane_iota() < HEAD_DIM
    w = 2 * SB_TQ

    @pl.when(qi == 0)
    def _():
        def prep(r0):
            rs = pl.ds(r0, PREP_ROWS)
            kb_s[rs, :] = k_ref[rs, :].astype(BF16)
            vt_s[:, rs] = v_ref[rs, :].T.astype(BF16)
            q = q_ref[rs, :] * (HEAD_DIM ** -0.5)
            qt_s[0, :, rs] = jnp.where(first, q, 0.0).T.astype(BF16)
            qt_s[1, :, rs] = jnp.where(first, 0.0, q).T.astype(BF16)

        _prep_loop(seq, prep)

    qs = pl.ds(pl.multiple_of(qi * SB_TQ, SB_TQ), SB_TQ)
    qt = jnp.concatenate([qt_s[0, :, qs], qt_s[1, :, qs]], axis=1)

    def chunk(k0, nk, drop, acc, strict):
        z = _dot(kb_s[pl.ds(k0, nk), :], qt)
        softplus = jnp.maximum(z, 0.0) + jnp.log(1.0 + jnp.exp(-jnp.abs(z)))
        log_b = z - softplus
        if strict is not None:
            softplus = jnp.where(strict, softplus, 0.0)
        hi = softplus.astype(BF16)
        lo = (softplus - hi.astype(F32)).astype(BF16)
        later = u_ref[0:nk, 0:nk]
        wgt = jnp.exp(log_b - (drop + (_dot(later, hi) + _dot(later, lo))))
        if strict is not None:
            wgt = jnp.where(strict, wgt, 0.0)
        acc = acc + _dot(vt_s[:, pl.ds(k0, nk)], wgt.astype(BF16))
        return drop + jnp.sum(softplus, axis=0, keepdims=True), acc

    k0 = pl.multiple_of(jnp.maximum(qi - 1, 0) * TK, TK)
    strict = _pair_causal(2 * TK, SB_TQ, qi * SB_TQ - k0, strict=True)
    rsum, acc = chunk(k0, 2 * TK, jnp.zeros((1, w), F32), jnp.zeros((LANES, w), F32), strict)

    def cond(c):
        return jnp.logical_and(c[0] >= 0, c[1] > 0)

    def alive(drop):
        return (jnp.min(drop) < -DEAD_LOG).astype(jnp.int32)

    def body(c):
        rs, ac = chunk(pl.multiple_of(c[0] * TK, TK), TK, c[2], c[3], None)
        return c[0] - 1, alive(rs), rs, ac

    _, _, _, acc = lax.while_loop(cond, body, (qi - 2, alive(rsum), rsum, acc))
    _store_pair(o_ref, acc[0:HEAD_DIM, 0:SB_TQ], acc[HEAD_DIM:LANES, SB_TQ:w])


def _sb_attention(p, qcol, b, s):
    assert SB_TQ == TK
    nq = s // SB_TQ
    r = lax.broadcasted_iota(jnp.int32, (2 * TK, 2 * TK), 0)
    c = lax.broadcasted_iota(jnp.int32, (2 * TK, 2 * TK), 1)
    later = (c > r).astype(BF16)
    return pl.pallas_call(
        functools.partial(_sb_kernel, seq=s),
        grid=(b, N_PAIRS, nq),
        in_specs=_pair_specs(s, qcol, qcol + N_PAIRS, qcol + 2 * N_PAIRS) + [
            pl.BlockSpec((2 * TK, 2 * TK), lambda b_, hp, qi: (0, 0)),
        ],
        out_specs=pl.BlockSpec((SB_TQ, LANES), lambda b_, hp, qi: (b_ * nq + qi, hp)),
        out_shape=jax.ShapeDtypeStruct((b * s, N_PAIRS * LANES), BF16),
        scratch_shapes=[pltpu.VMEM((2, LANES, s), BF16), pltpu.VMEM((s, LANES), BF16), pltpu.VMEM((LANES, s), BF16)],
        compiler_params=_cparams("parallel", "parallel", "arbitrary"),
        name="stick_breaking_attention",
    )(p, p, p, later)


def _t5_bucket(rel):
    n = jnp.maximum(rel, 0)
    nf = jnp.maximum(n, 1).astype(F32)
    large = REL_MAX_EXACT + (jnp.log(nf / REL_MAX_EXACT) / math.log(REL_MAX_DISTANCE / REL_MAX_EXACT)
                             * (REL_BUCKETS - REL_MAX_EXACT)).astype(jnp.int32)
    large = jnp.minimum(large, REL_BUCKETS - 1)
    return jnp.where(n < REL_MAX_EXACT, n, large)


def _bias_tile_kernel(rb_ref, own_ref, prev_ref):
    hp = pl.program_id(0)
    kidx, qidx = _key_query_iotas(MOBA_BLOCK, MOBA_BLOCK)
    for off, ref in ((0, own_ref), (MOBA_BLOCK, prev_ref)):
        bucket = _t5_bucket(qidx - kidx + off)
        for j in range(2):
            out = jnp.zeros((MOBA_BLOCK, MOBA_BLOCK), F32)
            for bk in range(REL_BUCKETS):
                out = jnp.where(bucket == bk, rb_ref[2 * hp + j, bk], out)
            ref[0, :, j * MOBA_BLOCK:(j + 1) * MOBA_BLOCK] = out - rb_ref[2 * hp + j, REL_BUCKETS - 1]


def _bias_tiles(rel_bias):
    shape = jax.ShapeDtypeStruct((N_PAIRS, MOBA_BLOCK, 2 * MOBA_BLOCK), F32)
    spec = pl.BlockSpec((1, MOBA_BLOCK, 2 * MOBA_BLOCK), lambda h: (h, 0, 0))
    return pl.pallas_call(
        _bias_tile_kernel,
        grid=(N_PAIRS,),
        in_specs=[pl.BlockSpec(memory_space=pltpu.SMEM)],
        out_specs=[spec, spec],
        out_shape=[shape, shape],
        compiler_params=_cparams("arbitrary"),
        name="moba_bias_tiles",
    )(rel_bias)


def _moba_kernel(q_ref, k_ref, v_ref, bown_ref, bprev_ref, gq_ref, gk_ref, o_ref,
                 qt_s, kn_s, vt_s, km_s, sel_s, s_s, p_s, *, seq):
    qi = pl.program_id(2)
    nb = seq // MOBA_BLOCK
    first = _lane_iota() < HEAD_DIM
    w = 2 * TQ
    per_step = TQ // MOBA_BLOCK
    per_group = KG // MOBA_BLOCK

    @pl.when(qi == 0)
    def _():
        def prep_keys(n, carry):
            rs = pl.ds(pl.multiple_of(n * MOBA_BLOCK, MOBA_BLOCK), MOBA_BLOCK)
            kn = _pair_rms(k_ref[rs, :], gk_ref[...])
            kn_s[rs, :] = kn.astype(BF16)
            km_s[pl.ds(n, 1), :] = jnp.mean(kn, axis=0, keepdims=True)
            vt = _transposed(v_ref[rs, :].astype(BF16))
            vt_s[0, :, rs] = _values_with_ones(vt).astype(BF16)
            vt_s[1, :, rs] = _values_with_ones(pltpu.roll(vt, HEAD_DIM, 0)).astype(BF16)
            return carry

        lax.fori_loop(0, nb, prep_keys, 0)

        def prep_queries(m, carry):
            blk = lax.broadcasted_iota(jnp.int32, (nb, 1), 0)
            blocks = [2 * m, 2 * m + 1]
            rows = [pl.ds(pl.multiple_of(n * MOBA_BLOCK, MOBA_BLOCK), MOBA_BLOCK) for n in blocks]
            qhs = []
            for rs in rows:
                qn = _pair_rms(q_ref[rs, :], gq_ref[...]) * (HEAD_DIM ** -0.5)
                qhs.append((jnp.where(first, qn, 0.0), jnp.where(first, 0.0, qn)))
            nt = (((1,), (1,)), ((), ()))
            gates = [jnp.concatenate([lax.dot_general(km_s[...], qh, nt, precision=HIGHEST, preferred_element_type=F32)
                                      for qh in qh2], axis=1) for qh2 in qhs]
            for n, rs, qh2, gate in zip(blocks, rows, qhs, gates):
                past = blk < n
                gate = jnp.where(past, gate, -jnp.inf)
                beaten = jnp.zeros((nb, 2 * MOBA_BLOCK), F32)
                for n2 in range(nb):
                    g2 = gate[n2:n2 + 1, :]
                    wins = jnp.logical_or(g2 > gate, jnp.logical_and(g2 == gate, n2 < blk))
                    beaten = beaten + jnp.where(wins, 1.0, 0.0)
                sel = jnp.where(jnp.logical_and(beaten < min(MOBA_TOPK, nb), past), 1.0, 0.0)
                for j in range(2):
                    qt_s[j, :, rs] = _transposed(qh2[j].astype(BF16)).astype(BF16)
                    sel_s[j, :, rs] = sel[:, j * MOBA_BLOCK:(j + 1) * MOBA_BLOCK]
            return carry

        lax.fori_loop(0, nb // 2, prep_queries, 0)

    qs = pl.ds(pl.multiple_of(qi * TQ, TQ), TQ)
    qt = jnp.concatenate([qt_s[0, :, qs], qt_s[1, :, qs]], axis=1)
    lane_w = lax.broadcasted_iota(jnp.int32, (1, w), 1)
    in_step = jnp.where(lane_w >= TQ, lane_w - TQ, lane_w)
    cur = qi * per_step + jnp.right_shift(in_step, MOBA_BLOCK.bit_length() - 1)

    def per_query_block(ref):
        return jnp.concatenate([ref[0, :, j * MOBA_BLOCK:(j + 1) * MOBA_BLOCK]
                                for j in range(2) for _ in range(per_step)], axis=1)

    def group_scores(g, diagonal):
        raw = _dot(kn_s[pl.ds(pl.multiple_of(g * KG, KG), KG), :], qt)
        pieces = []
        for i in range(per_group):
            n = g * per_group + i
            s = raw[i * MOBA_BLOCK:(i + 1) * MOBA_BLOCK, :]
            mask = jnp.concatenate([sel_s[0, pl.ds(n, 1), qs], sel_s[1, pl.ds(n, 1), qs]], axis=1) > 0.5
            follows = cur == n + 1
            if diagonal:
                own = cur == n
                kidx, qidx = _key_query_iotas(MOBA_BLOCK, w)
                causal = kidx <= jnp.bitwise_and(qidx, MOBA_BLOCK - 1)
                mask = jnp.logical_or(jnp.logical_and(own, causal), jnp.logical_and(jnp.logical_not(own), mask))
                s = s + jnp.where(own, per_query_block(bown_ref), jnp.where(follows, per_query_block(bprev_ref), 0.0))
            elif i == per_group - 1:
                s = s + jnp.where(follows, per_query_block(bprev_ref), 0.0)
            pieces.append(jnp.where(mask, s, -jnp.inf))
        return jnp.concatenate(pieces, axis=0)

    g0, _ = _diag_group(qi)
    _store_pair(o_ref, *_attend(group_scores(g0, True), g0, lambda g: group_scores(g, False), vt_s, s_s, p_s))


def _moba_attention(p, bown, bprev, gq, gk, b, s):
    assert KG % TQ == 0 and TQ % MOBA_BLOCK == 0 and s % KG == 0
    nq = s // TQ
    nb = s // MOBA_BLOCK
    gq2 = jnp.tile(gq, 2).reshape(1, LANES)
    gk2 = jnp.tile(gk, 2).reshape(1, LANES)
    bias_spec = pl.BlockSpec((1, MOBA_BLOCK, 2 * MOBA_BLOCK), lambda b_, hp, qi: (hp, 0, 0))
    return pl.pallas_call(
        functools.partial(_moba_kernel, seq=s),
        grid=(b, N_PAIRS, nq),
        in_specs=_pair_specs(s, 0, N_PAIRS, 2 * N_PAIRS) + [
            bias_spec, bias_spec,
            pl.BlockSpec((1, LANES), lambda b_, hp, qi: (0, 0)),
            pl.BlockSpec((1, LANES), lambda b_, hp, qi: (0, 0)),
        ],
        out_specs=pl.BlockSpec((TQ, LANES), lambda b_, hp, qi: (b_ * nq + qi, hp)),
        out_shape=jax.ShapeDtypeStruct((b * s, N_PAIRS * LANES), BF16),
        scratch_shapes=[pltpu.VMEM((2, LANES, s), BF16), pltpu.VMEM((s, LANES), BF16),
                        pltpu.VMEM((2, VT_ROWS, s), BF16), pltpu.VMEM((nb, LANES), F32),
                        pltpu.VMEM((2, nb, s), F32)] + _score_scratch(),
        compiler_params=_cparams("parallel", "parallel", "arbitrary"),
        name="moba_attention",
    )(p, p, p, bown, bprev, gq2, gk2)


def _rope(x, cos, sin):
    return x * cos + pltpu.roll(x, LANES - MLA_ROPE // 2, 1) * sin


def _mla_norm_rope(x, gain, cos, sin):
    ms = jnp.sum(jnp.where(_lane_iota() < MLA_QK, x * x, 0.0), axis=1, keepdims=True) * (1.0 / MLA_QK)
    return _rope(x * lax.rsqrt(ms + RMS_EPS) * gain, cos, sin)


def _mla_lanes(t):
    x1 = t[..., MLA_NOPE:MLA_NOPE + MLA_ROPE // 2]
    pad = jnp.zeros(t.shape[:-1] + (LANES - MLA_QK - MLA_ROPE // 2,), t.dtype)
    return jnp.concatenate([t, x1, pad], axis=-1)


def _mla_kernel(q_ref, kv_ref, kr_ref, cos_ref, sin_ref, gq_ref, gk_ref, o_ref, qt_s, kn_s, vt_s, s_s, p_s, *, seq):
    qi = pl.program_id(2)
    first = _lane_iota() < HEAD_DIM

    @pl.when(qi == 0)
    def _():
        def prep(r0):
            rs = pl.ds(r0, PREP_ROWS)
            cos, sin = cos_ref[rs, :], sin_ref[rs, :]
            for j in range(2):
                kv = kv_ref[rs, j * LANES:(j + 1) * LANES]
                k = jnp.where(first, kv, kr_ref[rs, :])
                kn_s[j, rs, :] = _mla_norm_rope(k, gk_ref[...], cos, sin).astype(BF16)
                vt_s[j, :, rs] = _values_with_ones(pltpu.roll(_transposed(kv.astype(BF16)), HEAD_DIM, 0)).astype(BF16)
                q = _mla_norm_rope(q_ref[rs, j * LANES:(j + 1) * LANES], gq_ref[...], cos, sin) * (MLA_QK ** -0.5)
                qt_s[j, :, rs] = _transposed(q.astype(BF16)).astype(BF16)

        _prep_loop(seq, prep)

    qs = pl.ds(pl.multiple_of(qi * TQ, TQ), TQ)
    qt = [qt_s[0, :, qs], qt_s[1, :, qs]]

    def scores(g):
        ks = pl.ds(pl.multiple_of(g * KG, KG), KG)
        return jnp.concatenate([_dot(kn_s[0, ks, :], qt[0]), _dot(kn_s[1, ks, :], qt[1])], axis=1)

    g0, shift = _diag_group(qi)
    diag = _pair_causal(KG, TQ, shift)
    _store_pair(o_ref, *_attend(jnp.where(diag, scores(g0), -jnp.inf), g0, scores, vt_s, s_s, p_s))


def _mla_attention(qd, kvd, p, kr_col, tables, gq, gk, b, s):
    nq = s // TQ
    cos, sin = tables
    gq128 = _mla_lanes(gq).reshape(1, LANES)
    gk128 = _mla_lanes(gk).reshape(1, LANES)
    full = pl.BlockSpec((s, LANES), lambda b_, hp, qi: (0, 0))
    gain = pl.BlockSpec((1, LANES), lambda b_, hp, qi: (0, 0))
    return pl.pallas_call(
        functools.partial(_mla_kernel, seq=s),
        grid=(b, N_PAIRS, nq),
        in_specs=[
            pl.BlockSpec((s, 2 * LANES), lambda b_, hp, qi: (b_, hp)),
            pl.BlockSpec((s, 2 * LANES), lambda b_, hp, qi: (b_, hp)),
            pl.BlockSpec((s, LANES), lambda b_, hp, qi: (b_, kr_col)),
            full, full, gain, gain,
        ],
        out_specs=pl.BlockSpec((TQ, LANES), lambda b_, hp, qi: (b_ * nq + qi, hp)),
        out_shape=jax.ShapeDtypeStruct((b * s, N_PAIRS * LANES), BF16),
        scratch_shapes=[pltpu.VMEM((2, LANES, s), BF16), pltpu.VMEM((2, s, LANES), BF16),
                        pltpu.VMEM((2, VT_ROWS, s), BF16)] + _score_scratch(),
        compiler_params=_cparams("parallel", "parallel", "arbitrary"),
        name="mla_attention",
    )(qd, kvd, p, cos, sin, gq128, gk128)


def _rope_tables(s):
    half = MLA_ROPE // 2
    inv_freq = ROPE_BASE ** (-jnp.arange(half, dtype=F32) / half)
    ang = jnp.arange(s, dtype=F32)[:, None] * inv_freq[None, :]
    cos, sin = jnp.cos(ang), jnp.sin(ang)
    z = lambda w: jnp.zeros((s, w), F32)
    cos_t = jnp.concatenate([jnp.ones((s, MLA_NOPE), F32), cos, cos, z(LANES - MLA_QK)], axis=1)
    sin_t = jnp.concatenate([z(MLA_NOPE), -sin, sin, z(LANES - MLA_QK)], axis=1)
    return cos_t, sin_t


def _even_mixer(x, b, s, norm, w_in, forget_bias, gq, gk):
    d = x.shape[1]
    w = HEAD_DIM * N_HEADS
    cuts = [0, w, 2 * w, 3 * w, 3 * w + N_HEADS, 4 * w + N_HEADS, 5 * w + N_HEADS, 6 * w + N_HEADS]
    qa, ka, va, fa, qb, kb, vb = (w_in[:, cuts[i]:cuts[i + 1]] for i in range(7))
    w_perm = jnp.concatenate([qa, ka, va, qb, kb, vb, fa, jnp.zeros((d, LANES - N_HEADS), F32)], axis=1).astype(BF16)
    p = _norm_matmul(x, 0, d, norm, w_perm)
    cdecay = _log_decay(p, 6 * N_PAIRS, forget_bias, b, s)
    out_a = _fox_attention(p, cdecay, gq, gk, b, s)
    out_b = _sb_attention(p, 3 * N_PAIRS, b, s)
    return out_a, out_b


def _odd_mixer(x, b, s, norm, w_in, gq_moba, gk_moba, q_a_norm, w_q_b, kv_a_norm, w_kv_b, gq_mla, gk_mla,
               bias_tiles, rope_tables):
    d = x.shape[1]
    n_main = 3 * HEAD_DIM * N_HEADS + MLA_Q_LORA + MLA_KV_LORA
    w_rope = _mla_lanes(jnp.concatenate([jnp.zeros((d, MLA_NOPE), F32), w_in[:, n_main:]], axis=1))
    w_perm = jnp.concatenate([w_in[:, :n_main], w_rope], axis=1).astype(BF16)
    p = _norm_matmul(x, 0, d, norm, w_perm)
    out_c = _moba_attention(p, bias_tiles[0], bias_tiles[1], gq_moba, gk_moba, b, s)
    lat0 = 3 * HEAD_DIM * N_HEADS
    w_q = _mla_lanes(w_q_b.reshape(MLA_Q_LORA, N_HEADS, MLA_QK))
    qd = _norm_matmul(p, lat0 // MLA_Q_LORA, MLA_Q_LORA, q_a_norm, w_q.reshape(MLA_Q_LORA, N_HEADS * LANES).astype(BF16))
    kvd = _norm_matmul(p, (lat0 + MLA_Q_LORA) // MLA_KV_LORA, MLA_KV_LORA, kv_a_norm, w_kv_b.astype(BF16))
    out_d = _mla_attention(qd, kvd, p, (n_main // LANES), rope_tables, gq_mla, gk_mla, b, s)
    return out_c, out_d


def kernel(x, ffn_norm, ffn_w_gate_up, ffn_w_down, rel_bias, ev_norm, ev_w_in, ev_forget_bias, ev_fox_q_norm, ev_fox_k_norm, ev_w_out, od_norm, od_w_in, od_moba_q_norm, od_moba_k_norm, od_mla_q_a_norm, od_mla_w_q_b, od_mla_kv_a_norm, od_mla_w_kv_b, od_mla_q_norm, od_mla_k_norm, od_w_out):
    b, s, d = x.shape
    depth = ffn_norm.shape[0]
    bias_tiles = _bias_tiles(rel_bias)
    rope_tables = _rope_tables(s)
    x = x.reshape(b * s, d)
    for layer in range(depth):
        i = layer // 2
        if layer % 2 == 0:
            mixed = _even_mixer(x, b, s, ev_norm[i], ev_w_in[i], ev_forget_bias[i], ev_fox_q_norm[i],
                                ev_fox_k_norm[i])
            w_out = ev_w_out[i]
        else:
            mixed = _odd_mixer(x, b, s, od_norm[i], od_w_in[i], od_moba_q_norm[i], od_moba_k_norm[i],
                               od_mla_q_a_norm[i], od_mla_w_q_b[i], od_mla_kv_a_norm[i], od_mla_w_kv_b[i],
                               od_mla_q_norm[i], od_mla_k_norm[i], bias_tiles, rope_tables)
            w_out = od_w_out[i]
        x = _mix_ffn(mixed[0], mixed[1], w_out, x, ffn_norm[layer], ffn_w_gate_up[layer], ffn_w_down[layer])
    return x.reshape(b, s, d)
```

```python
import functools
import math

import jax
import jax.numpy as jnp
from jax import lax
from jax.experimental import pallas as pl
from jax.experimental.pallas import tpu as pltpu

F32 = jnp.float32
BF16 = jnp.bfloat16

HEAD_DIM = 64
N_HEADS = 8
N_PAIRS = N_HEADS // 2
LANES = 128
MOBA_BLOCK = 256
MOBA_TOPK = 3
MLA_Q_LORA = 256
MLA_KV_LORA = 128
MLA_NOPE = 64
MLA_ROPE = 32
MLA_QK = MLA_NOPE + MLA_ROPE
ROPE_BASE = 10000.0
REL_BUCKETS = 32
REL_MAX_EXACT = 16
REL_MAX_DISTANCE = 128
RMS_EPS = 1e-6
TQ = 256
KG = 512
SB_TQ = 256
TK = 256
VT_ROWS = 80
PREP_ROWS = 512
DEAD_LOG = -110.0
VMEM_LIMIT = 56 * 1024 * 1024
ROW_TILE_BYTES = 12 * 1024 * 1024
MAX_ROW_TILE = 1024
HIGHEST = lax.Precision.HIGHEST


def _cparams(*sem):
    return pltpu.CompilerParams(dimension_semantics=sem, vmem_limit_bytes=VMEM_LIMIT)


def _dot(a, b, **kw):
    return jnp.dot(a, b, preferred_element_type=F32, **kw)


def _log_sigmoid(z):
    return jnp.minimum(z, 0.0) - jnp.log1p(jnp.exp(-jnp.abs(z)))


def _lane_iota():
    return lax.broadcasted_iota(jnp.int32, (1, LANES), 1)


def _row_iota():
    return lax.broadcasted_iota(jnp.int32, (LANES, 1), 0)


def _pair_rms(x, gain):
    first = _lane_iota() < HEAD_DIM
    sq = x * x
    s0 = jnp.sum(jnp.where(first, sq, 0.0), axis=1, keepdims=True)
    s1 = jnp.sum(jnp.where(first, 0.0, sq), axis=1, keepdims=True)
    ms = jnp.where(first, s0, s1) * (1.0 / HEAD_DIM)
    return x * lax.rsqrt(ms + RMS_EPS) * gain


def _transposed(x):
    r = lax.broadcasted_iota(jnp.int32, (LANES, LANES), 0)
    c = lax.broadcasted_iota(jnp.int32, (LANES, LANES), 1)
    return lax.dot_general((r == c).astype(BF16), x, (((1,), (1,)), ((), ())), preferred_element_type=F32)


def _split3(c):
    hi = c.astype(BF16).astype(F32)
    mid = (c - hi).astype(BF16).astype(F32)
    return hi, mid, c - hi - mid


def _norm_matmul_kernel(x_ref, g_ref, w_ref, o_ref, *, n_chunk):
    x = x_ref[...]
    ms = jnp.mean(x * x, axis=-1, keepdims=True)
    h = (x * lax.rsqrt(ms + RMS_EPS) * g_ref[...]).astype(BF16)
    n = o_ref.shape[1]
    for c0 in range(0, n, n_chunk):
        c1 = min(n, c0 + n_chunk)
        o_ref[:, c0:c1] = _dot(h, w_ref[:, c0:c1])


def _norm_matmul(xw, col_blk, k, gain, w, n_chunk=512):
    m = xw.shape[0]
    n = w.shape[1]
    tm = min(m, MAX_ROW_TILE, 1 << ((ROW_TILE_BYTES // (4 * (k + n))).bit_length() - 1))
    assert m % tm == 0
    return pl.pallas_call(
        functools.partial(_norm_matmul_kernel, n_chunk=n_chunk),
        grid=(m // tm,),
        in_specs=[
            pl.BlockSpec((tm, k), lambda i: (i, col_blk)),
            pl.BlockSpec((1, k), lambda i: (0, 0)),
            pl.BlockSpec((k, n), lambda i: (0, 0)),
        ],
        out_specs=pl.BlockSpec((tm, n), lambda i: (i, 0)),
        out_shape=jax.ShapeDtypeStruct((m, n), F32),
        compiler_params=_cparams("parallel"),
        name="norm_matmul",
    )(xw, gain.reshape(1, k), w)


def _mix_ffn_kernel(a_ref, b_ref, wa_ref, wb_ref, r_ref, g_ref, wg_ref, wu_ref, wd_ref, o_ref, x_s, h_s, acc_s):
    c = pl.program_id(1)

    @pl.when(c == 0)
    def _():
        x = r_ref[...] + (_dot(a_ref[...], wa_ref[...]) + _dot(b_ref[...], wb_ref[...]))
        x_s[...] = x
        ms = jnp.mean(x * x, axis=-1, keepdims=True)
        h_s[...] = (x * lax.rsqrt(ms + RMS_EPS) * g_ref[...]).astype(BF16)
        acc_s[...] = jnp.zeros_like(acc_s)

    h = h_s[...]
    g = _dot(h, wg_ref[...])
    u = _dot(h, wu_ref[...])
    act = (g * jax.nn.sigmoid(g) * u).astype(BF16)
    acc_s[...] += _dot(act, wd_ref[...])

    @pl.when(c == pl.num_programs(1) - 1)
    def _():
        o_ref[...] = x_s[...] + acc_s[...]


def _mix_ffn(a, b, w_out, res, gain, w_gate_up, w_down, tm=512, n_ff_chunks=2):
    m, d = res.shape
    ka = a.shape[1]
    d_ff = w_down.shape[0]
    tf = d_ff // n_ff_chunks
    assert tf * n_ff_chunks == d_ff and tf % LANES == 0
    wo = w_out.astype(BF16)
    wgu = w_gate_up.astype(BF16)
    wd = w_down.astype(BF16)
    return pl.pallas_call(
        _mix_ffn_kernel,
        grid=(m // tm, n_ff_chunks),
        in_specs=[
            pl.BlockSpec((tm, ka), lambda i, c: (i, 0)),
            pl.BlockSpec((tm, ka), lambda i, c: (i, 0)),
            pl.BlockSpec((ka, d), lambda i, c: (0, 0)),
            pl.BlockSpec((ka, d), lambda i, c: (1, 0)),
            pl.BlockSpec((tm, d), lambda i, c: (i, 0)),
            pl.BlockSpec((1, d), lambda i, c: (0, 0)),
            pl.BlockSpec((d, tf), lambda i, c: (0, c)),
            pl.BlockSpec((d, tf), lambda i, c: (0, n_ff_chunks + c)),
            pl.BlockSpec((tf, d), lambda i, c: (c, 0)),
        ],
        out_specs=pl.BlockSpec((tm, d), lambda i, c: (i, 0)),
        out_shape=jax.ShapeDtypeStruct((m, d), F32),
        scratch_shapes=[pltpu.VMEM((tm, d), F32), pltpu.VMEM((tm, d), BF16), pltpu.VMEM((tm, d), F32)],
        compiler_params=_cparams("parallel", "arbitrary"),
        name="mix_swiglu_ffn",
    )(a, b, wo, wo, res, gain.reshape(1, d), wgu, wgu, wd)


def _decay_kernel(f_ref, b_ref, c_ref, carry_s):
    @pl.when(pl.program_id(1) == 0)
    def _():
        carry_s[...] = jnp.zeros_like(carry_s)

    ts = f_ref.shape[0]
    logf = _log_sigmoid(f_ref[...] + b_ref[...])
    r = lax.broadcasted_iota(jnp.int32, (ts, ts), 0)
    c = lax.broadcasted_iota(jnp.int32, (ts, ts), 1)
    tri = (c <= r).astype(BF16)
    cs = carry_s[...]
    for term in _split3(logf):
        cs = cs + _dot(tri, term.astype(BF16))
    carry_s[...] = cs[ts - 1:ts, :]
    c_ref[...] = cs[:, 0:N_HEADS]


def _log_decay(p, col_blk, bias, b, s, ts=256):
    ns = s // ts
    bias128 = jnp.zeros((1, LANES), F32).at[0, :N_HEADS].set(bias)
    return pl.pallas_call(
        _decay_kernel,
        grid=(b, ns),
        in_specs=[
            pl.BlockSpec((ts, LANES), lambda bi, si: (bi * ns + si, col_blk)),
            pl.BlockSpec((1, LANES), lambda bi, si: (0, 0)),
        ],
        out_specs=pl.BlockSpec((ts, N_HEADS), lambda bi, si: (bi * ns + si, 0)),
        out_shape=jax.ShapeDtypeStruct((b * s, N_HEADS), F32),
        scratch_shapes=[pltpu.VMEM((1, LANES), F32)],
        compiler_params=_cparams("parallel", "arbitrary"),
        name="fox_log_decay",
    )(p, bias128)


def _attend(first_scores, first_group, scores_fn, vt_s, s_s, p_s, alive_fn=None):
    w = 2 * TQ
    s_s[0] = first_scores
    p_s[0] = jnp.zeros((KG, w), BF16)

    def values(g, pr, al, a0, a1):
        k0 = pl.multiple_of(g * KG, KG)
        a0 = al[:, 0:TQ] * a0 + _dot(vt_s[0, :, pl.ds(k0, KG)], pr[:, 0:TQ])
        a1 = al[:, TQ:w] * a1 + _dot(vt_s[1, :, pl.ds(k0, KG)], pr[:, TQ:w])
        return a0, a1

    def softmax(s, m):
        m_new = jnp.maximum(m, jnp.max(s, axis=0, keepdims=True))
        return m_new, jnp.exp(m - m_new), jnp.exp(s - m_new).astype(BF16)

    def cond(carry):
        return jnp.logical_and(carry[0] < first_group, carry[1] > 0)

    def body(carry):
        i, _, m, al, a0, a1 = carry
        par = lax.rem(i, 2)
        g = first_group - i
        pr_prev = p_s[par]
        s = s_s[par]
        a0, a1 = values(jnp.minimum(g + 1, first_group), pr_prev, al, a0, a1)
        s_next = scores_fn(g - 1)
        m_new, alpha, pr = softmax(s, m)
        s_s[1 - par] = s_next
        p_s[1 - par] = pr
        alive = jnp.int32(1) if alive_fn is None else alive_fn(g - 1, m_new).astype(jnp.int32)
        return i + 1, alive, m_new, alpha, a0, a1

    def cond2(carry):
        return jnp.logical_and(carry[0] + 1 < first_group, carry[1] > 0)

    def body2(carry):
        i, _, m, al, a0, a1 = carry
        par = lax.rem(i, 2)
        g = first_group - i
        pr_prev = p_s[par]
        s = s_s[par]
        a0, a1 = values(jnp.minimum(g + 1, first_group), pr_prev, al, a0, a1)
        s_1 = scores_fn(g - 1)
        m_0, alpha_0, pr_0 = softmax(s, m)
        a0, a1 = values(g, pr_0, alpha_0, a0, a1)
        s_2 = scores_fn(g - 2)
        m_1, alpha_1, pr_1 = softmax(s_1, m_0)
        s_s[par] = s_2
        p_s[par] = pr_1
        alive = jnp.int32(1) if alive_fn is None else alive_fn(g - 2, m_1).astype(jnp.int32)
        return i + 2, alive, m_1, alpha_1, a0, a1

    init = (jnp.int32(0), jnp.int32(1), jnp.full((1, w), -jnp.inf, F32), jnp.ones((1, w), F32),
            jnp.zeros((VT_ROWS, TQ), F32), jnp.zeros((VT_ROWS, TQ), F32))
    n, _, m, al, a0, a1 = lax.while_loop(cond, body, lax.while_loop(cond2, body2, init))
    par = lax.rem(n, 2)
    g = first_group - n
    a0, a1 = values(jnp.minimum(g + 1, first_group), p_s[par], al, a0, a1)
    _, alpha, pr = softmax(s_s[par], m)
    a0, a1 = values(g, pr, alpha, a0, a1)
    return (a0[0:HEAD_DIM, :] / a0[HEAD_DIM:HEAD_DIM + 1, :],
            a1[0:HEAD_DIM, :] / a1[HEAD_DIM:HEAD_DIM + 1, :])


def _score_scratch():
    return [pltpu.VMEM((2, KG, 2 * TQ), F32), pltpu.VMEM((2, KG, 2 * TQ), BF16)]


def _key_query_iotas(nk, nq):
    k = lax.broadcasted_iota(jnp.int32, (nk, nq), 0)
    q = lax.broadcasted_iota(jnp.int32, (nk, nq), 1)
    return k, q


def _pair_causal(nk, tq, shift=0, strict=False):
    kidx, qidx = _key_query_iotas(nk, 2 * tq)
    qidx = jnp.where(qidx >= tq, qidx - tq, qidx) + shift
    return kidx < qidx if strict else kidx <= qidx


def _values_with_ones(vt):
    return jnp.where(_row_iota() < HEAD_DIM, vt, 1.0)[0:VT_ROWS, :]


def _prep_loop(n_rows, fn):
    def body(i, carry):
        fn(pl.multiple_of(i * PREP_ROWS, PREP_ROWS))
        return carry

    lax.fori_loop(0, n_rows // PREP_ROWS, body, 0)


def _pair_specs(s, qcol, kcol, vcol):
    return [pl.BlockSpec((s, LANES), lambda b, hp, qi, col=col: (b, col + hp)) for col in (qcol, kcol, vcol)]


def _diag_group(qi):
    g0 = (qi * TQ + TQ - 1) // KG
    return g0, qi * TQ - g0 * KG


def _store_pair(o_ref, out0, out1):
    o_ref[...] = jnp.concatenate([out0, out1], axis=0).T.astype(o_ref.dtype)


def _fox_kernel(q_ref, k_ref, v_ref, c_ref, gq_ref, gk_ref, o_ref,
                qt_s, cq_s, ka_s, vt_s, s_s, p_s, *, seq):
    hp = pl.program_id(1)
    qi = pl.program_id(2)
    lane = _lane_iota()
    first = lane < HEAD_DIM
    head_lane = lax.broadcasted_iota(jnp.int32, (1, N_HEADS), 1)
    second = lax.broadcasted_iota(jnp.int32, (1, 2 * TQ), 1) >= TQ

    def augment(x, c, key_side):
        hi, mid, lo = _split3(c)
        sgn = -1.0 if key_side else 1.0
        c0 = HEAD_DIM + (3 if key_side else 0)
        one0 = HEAD_DIM + (0 if key_side else 3)
        out = jnp.where(first, x, 0.0)
        out = jnp.where(jnp.logical_and(lane >= one0, lane < one0 + 3), 1.0, out)
        for i, t in enumerate((hi, mid, lo)):
            out = jnp.where(lane == c0 + i, sgn * t, out)
        return out

    def head_col(cc, h):
        return jnp.sum(jnp.where(head_lane == h, cc, 0.0), axis=1, keepdims=True)

    @pl.when(qi == 0)
    def _():
        def prep(r0):
            rs = pl.ds(r0, PREP_ROWS)
            kn = _pair_rms(k_ref[rs, :], gk_ref[...])
            qn = _pair_rms(q_ref[rs, :], gq_ref[...]) * (HEAD_DIM ** -0.5)
            vt = v_ref[rs, :].T
            cc = c_ref[rs, :]
            for j in range(2):
                c = head_col(cc, 2 * hp + j)
                kj = kn if j == 0 else pltpu.roll(kn, HEAD_DIM, 1)
                ka_s[j, rs, :] = augment(kj, c, True).astype(BF16)
                vj = vt if j == 0 else pltpu.roll(vt, HEAD_DIM, 0)
                vt_s[j, :, rs] = _values_with_ones(vj).astype(BF16)
                qa = augment(qn if j == 0 else pltpu.roll(qn, HEAD_DIM, 1), c, False).T
                qt_s[j, :, rs] = qa.astype(BF16)
                cq_s[j, 0:1, rs] = jnp.sum(qa[HEAD_DIM:HEAD_DIM + 3, :], axis=0, keepdims=True)

        _prep_loop(seq, prep)

    qs = pl.ds(pl.multiple_of(qi * TQ, TQ), TQ)
    qt = [qt_s[0, :, qs], qt_s[1, :, qs]]
    qk_bound = 1.02 * HEAD_DIM ** 0.5 * (jnp.max(jnp.abs(gq_ref[...]), axis=1, keepdims=True)
                                          * jnp.max(jnp.abs(gk_ref[...]), axis=1, keepdims=True))
    score_bound = qk_bound + jnp.concatenate([cq_s[0, 0:1, qs], cq_s[1, 0:1, qs]], axis=1)

    def scores(g):
        ks = pl.ds(pl.multiple_of(g * KG, KG), KG)
        return jnp.concatenate([_dot(ka_s[0, ks, :], qt[0]), _dot(ka_s[1, ks, :], qt[1])], axis=1)

    def alive(g, m):
        row = c_ref[pl.ds((g + 1) * KG - 1, 1), :]
        c0 = jnp.sum(jnp.where(head_lane == 2 * hp, row, 0.0), axis=1, keepdims=True)
        c1 = jnp.sum(jnp.where(head_lane == 2 * hp + 1, row, 0.0), axis=1, keepdims=True)
        return jnp.max(score_bound - jnp.where(second, c1, c0) - m) > DEAD_LOG

    g0, shift = _diag_group(qi)
    diag = _pair_causal(KG, TQ, shift)
    _store_pair(o_ref, *_attend(jnp.where(diag, scores(g0), -jnp.inf), g0, scores, vt_s, s_s, p_s, alive))


def _fox_attention(p, cdecay, gq, gk, b, s):
    nq = s // TQ
    gq2 = jnp.tile(gq, 2).reshape(1, LANES)
    gk2 = jnp.tile(gk, 2).reshape(1, LANES)
    return pl.pallas_call(
        functools.partial(_fox_kernel, seq=s),
        grid=(b, N_PAIRS, nq),
        in_specs=_pair_specs(s, 0, N_PAIRS, 2 * N_PAIRS) + [
            pl.BlockSpec((s, N_HEADS), lambda b_, hp, qi: (b_, 0)),
            pl.BlockSpec((1, LANES), lambda b_, hp, qi: (0, 0)),
            pl.BlockSpec((1, LANES), lambda b_, hp, qi: (0, 0)),
        ],
        out_specs=pl.BlockSpec((TQ, LANES), lambda b_, hp, qi: (b_ * nq + qi, hp)),
        out_shape=jax.ShapeDtypeStruct((b * s, N_PAIRS * LANES), BF16),
        scratch_shapes=[pltpu.VMEM((2, LANES, s), BF16), pltpu.VMEM((2, 8, s), F32),
                        pltpu.VMEM((2, s, LANES), BF16), pltpu.VMEM((2, VT_ROWS, s), BF16)] + _score_scratch(),
        compiler_params=_cparams("parallel", "parallel", "arbitrary"),
        name="fox_attention",
    )(p, p, p, cdecay, gq2, gk2)


def _sb_kernel(q_ref, k_ref, v_ref, u_ref, o_ref, qt_s, kb_s, vt_s, *, seq):
    qi = pl.program_id(2)
    first = _lane_iota() < HEAD_DIM
    w = 2 * SB_TQ

    @pl.when(qi == 0)
    def _():
        def prep(r0):
            rs = pl.ds(r0, PREP_ROWS)
            kb_s[rs, :] = k_ref[rs, :].astype(BF16)
            vt_s[:, rs] = v_ref[rs, :].T.astype(BF16)
            q = q_ref[rs, :] * (HEAD_DIM ** -0.5)
            qt_s[0, :, rs] = jnp.where(first, q, 0.0).T.astype(BF16)
            qt_s[1, :, rs] = jnp.where(first, 0.0, q).T.astype(BF16)

        _prep_loop(seq, prep)

    qs = pl.ds(pl.multiple_of(qi * SB_TQ, SB_TQ), SB_TQ)
    qt = jnp.concatenate([qt_s[0, :, qs], qt_s[1, :, qs]], axis=1)

    def chunk(k0, nk, drop, acc, strict):
        z = _dot(kb_s[pl.ds(k0, nk), :], qt)
        softplus = jnp.maximum(z, 0.0) + jnp.log(1.0 + jnp.exp(-jnp.abs(z)))
        log_b = z - softplus
        if strict is not None:
            softplus = jnp.where(strict, softplus, 0.0)
        hi = softplus.astype(BF16)
        lo = (softplus - hi.astype(F32)).astype(BF16)
        later = u_ref[0:nk, 0:nk]
        wgt = jnp.exp(log_b - (drop + (_dot(later, hi) + _dot(later, lo))))
        if strict is not None:
            wgt = jnp.where(strict, wgt, 0.0)
        acc = acc + _dot(vt_s[:, pl.ds(k0, nk)], wgt.astype(BF16))
        return drop + jnp.sum(softplus, axis=0, keepdims=True), acc

    k0 = pl.multiple_of(jnp.maximum(qi - 1, 0) * TK, TK)
    strict = _pair_causal(2 * TK, SB_TQ, qi * SB_TQ - k0, strict=True)
    rsum, acc = chunk(k0, 2 * TK, jnp.zeros((1, w), F32), jnp.zeros((LANES, w), F32), strict)

    def cond(c):
        return jnp.logical_and(c[0] >= 0, c[1] > 0)

    def alive(drop):
        return (jnp.min(drop) < -DEAD_LOG).astype(jnp.int32)

    def body(c):
        rs, ac = chunk(pl.multiple_of(c[0] * TK, TK), TK, c[2], c[3], None)
        return c[0] - 1, alive(rs), rs, ac

    _, _, _, acc = lax.while_loop(cond, body, (qi - 2, alive(rsum), rsum, acc))
    _store_pair(o_ref, acc[0:HEAD_DIM, 0:SB_TQ], acc[HEAD_DIM:LANES, SB_TQ:w])


def _sb_attention(p, qcol, b, s):
    assert SB_TQ == TK
    nq = s // SB_TQ
    r = lax.broadcasted_iota(jnp.int32, (2 * TK, 2 * TK), 0)
    c = lax.broadcasted_iota(jnp.int32, (2 * TK, 2 * TK), 1)
    later = (c > r).astype(BF16)
    return pl.pallas_call(
        functools.partial(_sb_kernel, seq=s),
        grid=(b, N_PAIRS, nq),
        in_specs=_pair_specs(s, qcol, qcol + N_PAIRS, qcol + 2 * N_PAIRS) + [
            pl.BlockSpec((2 * TK, 2 * TK), lambda b_, hp, qi: (0, 0)),
        ],
        out_specs=pl.BlockSpec((SB_TQ, LANES), lambda b_, hp, qi: (b_ * nq + qi, hp)),
        out_shape=jax.ShapeDtypeStruct((b * s, N_PAIRS * LANES), BF16),
        scratch_shapes=[pltpu.VMEM((2, LANES, s), BF16), pltpu.VMEM((s, LANES), BF16), pltpu.VMEM((LANES, s), BF16)],
        compiler_params=_cparams("parallel", "parallel", "arbitrary"),
        name="stick_breaking_attention",
    )(p, p, p, later)


def _t5_bucket(rel):
    n = jnp.maximum(rel, 0)
    nf = jnp.maximum(n, 1).astype(F32)
    large = REL_MAX_EXACT + (jnp.log(nf / REL_MAX_EXACT) / math.log(REL_MAX_DISTANCE / REL_MAX_EXACT)
                             * (REL_BUCKETS - REL_MAX_EXACT)).astype(jnp.int32)
    large = jnp.minimum(large, REL_BUCKETS - 1)
    return jnp.where(n < REL_MAX_EXACT, n, large)


def _bias_tile_kernel(rb_ref, own_ref, prev_ref):
    hp = pl.program_id(0)
    kidx, qidx = _key_query_iotas(MOBA_BLOCK, MOBA_BLOCK)
    for off, ref in ((0, own_ref), (MOBA_BLOCK, prev_ref)):
        bucket = _t5_bucket(qidx - kidx + off)
        for j in range(2):
            out = jnp.zeros((MOBA_BLOCK, MOBA_BLOCK), F32)
            for bk in range(REL_BUCKETS):
                out = jnp.where(bucket == bk, rb_ref[2 * hp + j, bk], out)
            ref[0, :, j * MOBA_BLOCK:(j + 1) * MOBA_BLOCK] = out - rb_ref[2 * hp + j, REL_BUCKETS - 1]


def _bias_tiles(rel_bias):
    shape = jax.ShapeDtypeStruct((N_PAIRS, MOBA_BLOCK, 2 * MOBA_BLOCK), F32)
    spec = pl.BlockSpec((1, MOBA_BLOCK, 2 * MOBA_BLOCK), lambda h: (h, 0, 0))
    return pl.pallas_call(
        _bias_tile_kernel,
        grid=(N_PAIRS,),
        in_specs=[pl.BlockSpec(memory_space=pltpu.SMEM)],
        out_specs=[spec, spec],
        out_shape=[shape, shape],
        compiler_params=_cparams("arbitrary"),
        name="moba_bias_tiles",
    )(rel_bias)


def _moba_kernel(q_ref, k_ref, v_ref, bown_ref, bprev_ref, gq_ref, gk_ref, o_ref,
                 qt_s, kn_s, vt_s, km_s, sel_s, s_s, p_s, *, seq):
    qi = pl.program_id(2)
    nb = seq // MOBA_BLOCK
    first = _lane_iota() < HEAD_DIM
    w = 2 * TQ
    per_step = TQ // MOBA_BLOCK
    per_group = KG // MOBA_BLOCK

    @pl.when(qi == 0)
    def _():
        def prep_keys(n, carry):
            rs = pl.ds(pl.multiple_of(n * MOBA_BLOCK, MOBA_BLOCK), MOBA_BLOCK)
            kn = _pair_rms(k_ref[rs, :], gk_ref[...])
            kn_s[rs, :] = kn.astype(BF16)
            km_s[pl.ds(n, 1), :] = jnp.mean(kn, axis=0, keepdims=True)
            vt = _transposed(v_ref[rs, :].astype(BF16))
            vt_s[0, :, rs] = _values_with_ones(vt).astype(BF16)
            vt_s[1, :, rs] = _values_with_ones(pltpu.roll(vt, HEAD_DIM, 0)).astype(BF16)
            return carry

        lax.fori_loop(0, nb, prep_keys, 0)

        def prep_queries(m, carry):
            blk = lax.broadcasted_iota(jnp.int32, (nb, 1), 0)
            blocks = [2 * m, 2 * m + 1]
            rows = [pl.ds(pl.multiple_of(n * MOBA_BLOCK, MOBA_BLOCK), MOBA_BLOCK) for n in blocks]
            qhs = []
            for rs in rows:
                qn = _pair_rms(q_ref[rs, :], gq_ref[...]) * (HEAD_DIM ** -0.5)
                qhs.append((jnp.where(first, qn, 0.0), jnp.where(first, 0.0, qn)))
            nt = (((1,), (1,)), ((), ()))
            gates = [jnp.concatenate([lax.dot_general(km_s[...], qh, nt, precision=HIGHEST, preferred_element_type=F32)
                                      for qh in qh2], axis=1) for qh2 in qhs]
            for n, rs, qh2, gate in zip(blocks, rows, qhs, gates):
                past = blk < n
                gate = jnp.where(past, gate, -jnp.inf)
                beaten = jnp.zeros((nb, 2 * MOBA_BLOCK), F32)
                for n2 in range(nb):
                    g2 = gate[n2:n2 + 1, :]
                    wins = jnp.logical_or(g2 > gate, jnp.logical_and(g2 == gate, n2 < blk))
                    beaten = beaten + jnp.where(wins, 1.0, 0.0)
                sel = jnp.where(jnp.logical_and(beaten < min(MOBA_TOPK, nb), past), 1.0, 0.0)
                for j in range(2):
                    qt_s[j, :, rs] = _transposed(qh2[j].astype(BF16)).astype(BF16)
                    sel_s[j, :, rs] = sel[:, j * MOBA_BLOCK:(j + 1) * MOBA_BLOCK]
            return carry

        lax.fori_loop(0, nb // 2, prep_queries, 0)

    qs = pl.ds(pl.multiple_of(qi * TQ, TQ), TQ)
    qt = jnp.concatenate([qt_s[0, :, qs], qt_s[1, :, qs]], axis=1)
    lane_w = lax.broadcasted_iota(jnp.int32, (1, w), 1)
    in_step = jnp.where(lane_w >= TQ, lane_w - TQ, lane_w)
    cur = qi * per_step + jnp.right_shift(in_step, MOBA_BLOCK.bit_length() - 1)

    def per_query_block(ref):
        return jnp.concatenate([ref[0, :, j * MOBA_BLOCK:(j + 1) * MOBA_BLOCK]
                                for j in range(2) for _ in range(per_step)], axis=1)

    def group_scores(g, diagonal):
        raw = _dot(kn_s[pl.ds(pl.multiple_of(g * KG, KG), KG), :], qt)
        pieces = []
        for i in range(per_group):
            n = g * per_group + i
            s = raw[i * MOBA_BLOCK:(i + 1) * MOBA_BLOCK, :]
            mask = jnp.concatenate([sel_s[0, pl.ds(n, 1), qs], sel_s[1, pl.ds(n, 1), qs]], axis=1) > 0.5
            follows = cur == n + 1
            if diagonal:
                own = cur == n
                kidx, qidx = _key_query_iotas(MOBA_BLOCK, w)
                causal = kidx <= jnp.bitwise_and(qidx, MOBA_BLOCK - 1)
                mask = jnp.logical_or(jnp.logical_and(own, causal), jnp.logical_and(jnp.logical_not(own), mask))
                s = s + jnp.where(own, per_query_block(bown_ref), jnp.where(follows, per_query_block(bprev_ref), 0.0))
            elif i == per_group - 1:
                s = s + jnp.where(follows, per_query_block(bprev_ref), 0.0)
            pieces.append(jnp.where(mask, s, -jnp.inf))
        return jnp.concatenate(pieces, axis=0)

    g0, _ = _diag_group(qi)
    _store_pair(o_ref, *_attend(group_scores(g0, True), g0, lambda g: group_scores(g, False), vt_s, s_s, p_s))


def _moba_attention(p, bown, bprev, gq, gk, b, s):
    assert KG % TQ == 0 and TQ % MOBA_BLOCK == 0 and s % KG == 0
    nq = s // TQ
    nb = s // MOBA_BLOCK
    gq2 = jnp.tile(gq, 2).reshape(1, LANES)
    gk2 = jnp.tile(gk, 2).reshape(1, LANES)
    bias_spec = pl.BlockSpec((1, MOBA_BLOCK, 2 * MOBA_BLOCK), lambda b_, hp, qi: (hp, 0, 0))
    return pl.pallas_call(
        functools.partial(_moba_kernel, seq=s),
        grid=(b, N_PAIRS, nq),
        in_specs=_pair_specs(s, 0, N_PAIRS, 2 * N_PAIRS) + [
            bias_spec, bias_spec,
            pl.BlockSpec((1, LANES), lambda b_, hp, qi: (0, 0)),
            pl.BlockSpec((1, LANES), lambda b_, hp, qi: (0, 0)),
        ],
        out_specs=pl.BlockSpec((TQ, LANES), lambda b_, hp, qi: (b_ * nq + qi, hp)),
        out_shape=jax.ShapeDtypeStruct((b * s, N_PAIRS * LANES), BF16),
        scratch_shapes=[pltpu.VMEM((2, LANES, s), BF16), pltpu.VMEM((s, LANES), BF16),
                        pltpu.VMEM((2, VT_ROWS, s), BF16), pltpu.VMEM((nb, LANES), F32),
                        pltpu.VMEM((2, nb, s), F32)] + _score_scratch(),
        compiler_params=_cparams("parallel", "parallel", "arbitrary"),
        name="moba_attention",
    )(p, p, p, bown, bprev, gq2, gk2)


def _rope(x, cos, sin):
    return x * cos + pltpu.roll(x, LANES - MLA_ROPE // 2, 1) * sin


def _mla_norm_rope(x, gain, cos, sin):
    ms = jnp.sum(jnp.where(_lane_iota() < MLA_QK, x * x, 0.0), axis=1, keepdims=True) * (1.0 / MLA_QK)
    return _rope(x * lax.rsqrt(ms + RMS_EPS) * gain, cos, sin)


def _mla_lanes(t):
    x1 = t[..., MLA_NOPE:MLA_NOPE + MLA_ROPE // 2]
    pad = jnp.zeros(t.shape[:-1] + (LANES - MLA_QK - MLA_ROPE // 2,), t.dtype)
    return jnp.concatenate([t, x1, pad], axis=-1)


def _mla_kernel(q_ref, kv_ref, kr_ref, cos_ref, sin_ref, gq_ref, gk_ref, o_ref, qt_s, kn_s, vt_s, s_s, p_s, *, seq):
    qi = pl.program_id(2)
    first = _lane_iota() < HEAD_DIM

    @pl.when(qi == 0)
    def _():
        def prep(r0):
            rs = pl.ds(r0, PREP_ROWS)
            cos, sin = cos_ref[rs, :], sin_ref[rs, :]
            for j in range(2):
                kv = kv_ref[rs, j * LANES:(j + 1) * LANES]
                k = jnp.where(first, kv, kr_ref[rs, :])
                kn_s[j, rs, :] = _mla_norm_rope(k, gk_ref[...], cos, sin).astype(BF16)
                vt_s[j, :, rs] = _values_with_ones(pltpu.roll(_transposed(kv.astype(BF16)), HEAD_DIM, 0)).astype(BF16)
                q = _mla_norm_rope(q_ref[rs, j * LANES:(j + 1) * LANES], gq_ref[...], cos, sin) * (MLA_QK ** -0.5)
                qt_s[j, :, rs] = _transposed(q.astype(BF16)).astype(BF16)

        _prep_loop(seq, prep)

    qs = pl.ds(pl.multiple_of(qi * TQ, TQ), TQ)
    qt = [qt_s[0, :, qs], qt_s[1, :, qs]]

    def scores(g):
        ks = pl.ds(pl.multiple_of(g * KG, KG), KG)
        return jnp.concatenate([_dot(kn_s[0, ks, :], qt[0]), _dot(kn_s[1, ks, :], qt[1])], axis=1)

    g0, shift = _diag_group(qi)
    diag = _pair_causal(KG, TQ, shift)
    _store_pair(o_ref, *_attend(jnp.where(diag, scores(g0), -jnp.inf), g0, scores, vt_s, s_s, p_s))


def _mla_attention(qd, kvd, p, kr_col, tables, gq, gk, b, s):
    nq = s // TQ
    cos, sin = tables
    gq128 = _mla_lanes(gq).reshape(1, LANES)
    gk128 = _mla_lanes(gk).reshape(1, LANES)
    full = pl.BlockSpec((s, LANES), lambda b_, hp, qi: (0, 0))
    gain = pl.BlockSpec((1, LANES), lambda b_, hp, qi: (0, 0))
    return pl.pallas_call(
        functools.partial(_mla_kernel, seq=s),
        grid=(b, N_PAIRS, nq),
        in_specs=[
            pl.BlockSpec((s, 2 * LANES), lambda b_, hp, qi: (b_, hp)),
            pl.BlockSpec((s, 2 * LANES), lambda b_, hp, qi: (b_, hp)),
            pl.BlockSpec((s, LANES), lambda b_, hp, qi: (b_, kr_col)),
            full, full, gain, gain,
        ],
        out_specs=pl.BlockSpec((TQ, LANES), lambda b_, hp, qi: (b_ * nq + qi, hp)),
        out_shape=jax.ShapeDtypeStruct((b * s, N_PAIRS * LANES), BF16),
        scratch_shapes=[pltpu.VMEM((2, LANES, s), BF16), pltpu.VMEM((2, s, LANES), BF16),
                        pltpu.VMEM((2, VT_ROWS, s), BF16)] + _score_scratch(),
        compiler_params=_cparams("parallel", "parallel", "arbitrary"),
        name="mla_attention",
    )(qd, kvd, p, cos, sin, gq128, gk128)


def _rope_tables(s):
    half = MLA_ROPE // 2
    inv_freq = ROPE_BASE ** (-jnp.arange(half, dtype=F32) / half)
    ang = jnp.arange(s, dtype=F32)[:, None] * inv_freq[None, :]
    cos, sin = jnp.cos(ang), jnp.sin(ang)
    z = lambda w: jnp.zeros((s, w), F32)
    cos_t = jnp.concatenate([jnp.ones((s, MLA_NOPE), F32), cos, cos, z(LANES - MLA_QK)], axis=1)
    sin_t = jnp.concatenate([z(MLA_NOPE), -sin, sin, z(LANES - MLA_QK)], axis=1)
    return cos_t, sin_t


def _even_mixer(x, b, s, norm, w_in, forget_bias, gq, gk):
    d = x.shape[1]
    w = HEAD_DIM * N_HEADS
    cuts = [0, w, 2 * w, 3 * w, 3 * w + N_HEADS, 4 * w + N_HEADS, 5 * w + N_HEADS, 6 * w + N_HEADS]
    qa, ka, va, fa, qb, kb, vb = (w_in[:, cuts[i]:cuts[i + 1]] for i in range(7))
    w_perm = jnp.concatenate([qa, ka, va, qb, kb, vb, fa, jnp.zeros((d, LANES - N_HEADS), F32)], axis=1).astype(BF16)
    p = _norm_matmul(x, 0, d, norm, w_perm)
    cdecay = _log_decay(p, 6 * N_PAIRS, forget_bias, b, s)
    out_a = _fox_attention(p, cdecay, gq, gk, b, s)
    out_b = _sb_attention(p, 3 * N_PAIRS, b, s)
    return out_a, out_b


def _odd_mixer(x, b, s, norm, w_in, gq_moba, gk_moba, q_a_norm, w_q_b, kv_a_norm, w_kv_b, gq_mla, gk_mla,
               bias_tiles, rope_tables):
    d = x.shape[1]
    n_main = 3 * HEAD_DIM * N_HEADS + MLA_Q_LORA + MLA_KV_LORA
    w_rope = _mla_lanes(jnp.concatenate([jnp.zeros((d, MLA_NOPE), F32), w_in[:, n_main:]], axis=1))
    w_perm = jnp.concatenate([w_in[:, :n_main], w_rope], axis=1).astype(BF16)
    p = _norm_matmul(x, 0, d, norm, w_perm)
    out_c = _moba_attention(p, bias_tiles[0], bias_tiles[1], gq_moba, gk_moba, b, s)
    lat0 = 3 * HEAD_DIM * N_HEADS
    w_q = _mla_lanes(w_q_b.reshape(MLA_Q_LORA, N_HEADS, MLA_QK))
    qd = _norm_matmul(p, lat0 // MLA_Q_LORA, MLA_Q_LORA, q_a_norm, w_q.reshape(MLA_Q_LORA, N_HEADS * LANES).astype(BF16))
    kvd = _norm_matmul(p, (lat0 + MLA_Q_LORA) // MLA_KV_LORA, MLA_KV_LORA, kv_a_norm, w_kv_b.astype(BF16))
    out_d = _mla_attention(qd, kvd, p, (n_main // LANES), rope_tables, gq_mla, gk_mla, b, s)
    return out_c, out_d


def kernel(x, ffn_norm, ffn_w_gate_up, ffn_w_down, rel_bias, ev_norm, ev_w_in, ev_forget_bias, ev_fox_q_norm, ev_fox_k_norm, ev_w_out, od_norm, od_w_in, od_moba_q_norm, od_moba_k_norm, od_mla_q_a_norm, od_mla_w_q_b, od_mla_kv_a_norm, od_mla_w_kv_b, od_mla_q_norm, od_mla_k_norm, od_w_out):
    b, s, d = x.shape
    depth = ffn_norm.shape[0]
    bias_tiles = _bias_tiles(rel_bias)
    rope_tables = _rope_tables(s)
    x = x.reshape(b * s, d)
    for layer in range(depth):
        i = layer // 2
        if layer % 2 == 0:
            mixed = _even_mixer(x, b, s, ev_norm[i], ev_w_in[i], ev_forget_bias[i], ev_fox_q_norm[i],
                                ev_fox_k_norm[i])
            w_out = ev_w_out[i]
        else:
            mixed = _odd_mixer(x, b, s, od_norm[i], od_w_in[i], od_moba_q_norm[i], od_moba_k_norm[i],
                               od_mla_q_a_norm[i], od_mla_w_q_b[i], od_mla_kv_a_norm[i], od_mla_w_kv_b[i],
                               od_mla_q_norm[i], od_mla_k_norm[i], bias_tiles, rope_tables)
            w_out = od_w_out[i]
        x = _mix_ffn(mixed[0], mixed[1], w_out, x, ffn_norm[layer], ffn_w_gate_up[layer], ffn_w_down[layer])
    return x.reshape(b, s, d)
```

```python
import functools
import math

import jax
import jax.numpy as jnp
from jax import lax
from jax.experimental import pallas as pl
from jax.experimental.pallas import tpu as pltpu

F32 = jnp.float32
BF16 = jnp.bfloat16

HEAD_DIM = 64
N_HEADS = 8
N_PAIRS = N_HEADS // 2
LANES = 128
MOBA_BLOCK = 256
MOBA_TOPK = 3
MLA_Q_LORA = 256
MLA_KV_LORA = 128
MLA_NOPE = 64
MLA_ROPE = 32
MLA_QK = MLA_NOPE + MLA_ROPE
ROPE_BASE = 10000.0
REL_BUCKETS = 32
REL_MAX_EXACT = 16
REL_MAX_DISTANCE = 128
RMS_EPS = 1e-6
TQ = 256
KG = 512
SB_TQ = 256
TK = 256
VT_ROWS = 80
PREP_ROWS = 512
DEAD_LOG = -110.0
VMEM_LIMIT = 56 * 1024 * 1024
ROW_TILE_BYTES = 12 * 1024 * 1024
MAX_ROW_TILE = 1024
HIGHEST = lax.Precision.HIGHEST


def _cparams(*sem):
    return pltpu.CompilerParams(dimension_semantics=sem, vmem_limit_bytes=VMEM_LIMIT)


def _attention_cparams():
    return pltpu.CompilerParams(dimension_semantics=("parallel", "parallel", "arbitrary"),
                                vmem_limit_bytes=VMEM_LIMIT)


def _dot(a, b, **kw):
    return jnp.dot(a, b, preferred_element_type=F32, **kw)


def _log_sigmoid(z):
    return jnp.minimum(z, 0.0) - jnp.log1p(jnp.exp(-jnp.abs(z)))


def _lane_iota():
    return lax.broadcasted_iota(jnp.int32, (1, LANES), 1)


def _row_iota():
    return lax.broadcasted_iota(jnp.int32, (LANES, 1), 0)


def _pair_rms(x, gain):
    first = _lane_iota() < HEAD_DIM
    sq = x * x
    s0 = jnp.sum(jnp.where(first, sq, 0.0), axis=1, keepdims=True)
    s1 = jnp.sum(jnp.where(first, 0.0, sq), axis=1, keepdims=True)
    ms = jnp.where(first, s0, s1) * (1.0 / HEAD_DIM)
    return x * lax.rsqrt(ms + RMS_EPS) * gain


def _transposed(x):
    r = lax.broadcasted_iota(jnp.int32, (LANES, LANES), 0)
    c = lax.broadcasted_iota(jnp.int32, (LANES, LANES), 1)
    return lax.dot_general((r == c).astype(BF16), x, (((1,), (1,)), ((), ())), preferred_element_type=F32)


def _split3(c):
    hi = c.astype(BF16).astype(F32)
    mid = (c - hi).astype(BF16).astype(F32)
    return hi, mid, c - hi - mid


def _norm_matmul_kernel(x_ref, g_ref, w_ref, o_ref, *, n_chunk):
    x = x_ref[...]
    ms = jnp.mean(x * x, axis=-1, keepdims=True)
    h = (x * lax.rsqrt(ms + RMS_EPS) * g_ref[...]).astype(BF16)
    n = o_ref.shape[1]
    for c0 in range(0, n, n_chunk):
        c1 = min(n, c0 + n_chunk)
        o_ref[:, c0:c1] = _dot(h, w_ref[:, c0:c1])


def _norm_matmul(xw, col_blk, k, gain, w, n_chunk=512):
    m = xw.shape[0]
    n = w.shape[1]
    tm = min(m, MAX_ROW_TILE, 1 << ((ROW_TILE_BYTES // (4 * (k + n))).bit_length() - 1))
    assert m % tm == 0
    return pl.pallas_call(
        functools.partial(_norm_matmul_kernel, n_chunk=n_chunk),
        grid=(m // tm,),
        in_specs=[
            pl.BlockSpec((tm, k), lambda i: (i, col_blk)),
            pl.BlockSpec((1, k), lambda i: (0, 0)),
            pl.BlockSpec((k, n), lambda i: (0, 0)),
        ],
        out_specs=pl.BlockSpec((tm, n), lambda i: (i, 0)),
        out_shape=jax.ShapeDtypeStruct((m, n), F32),
        compiler_params=_cparams("parallel"),
        name="norm_matmul",
    )(xw, gain.reshape(1, k), w)


def _mix_ffn_kernel(a_ref, b_ref, wa_ref, wb_ref, r_ref, g_ref, wg_ref, wu_ref, wd_ref, o_ref, x_s, h_s, acc_s):
    c = pl.program_id(1)

    @pl.when(c == 0)
    def _():
        x = r_ref[...] + (_dot(a_ref[...], wa_ref[...]) + _dot(b_ref[...], wb_ref[...]))
        x_s[...] = x
        ms = jnp.mean(x * x, axis=-1, keepdims=True)
        h_s[...] = (x * lax.rsqrt(ms + RMS_EPS) * g_ref[...]).astype(BF16)
        acc_s[...] = jnp.zeros_like(acc_s)

    h = h_s[...]
    g = _dot(h, wg_ref[...])
    u = _dot(h, wu_ref[...])
    act = (g * jax.nn.sigmoid(g) * u).astype(BF16)
    acc_s[...] += _dot(act, wd_ref[...])

    @pl.when(c == pl.num_programs(1) - 1)
    def _():
        o_ref[...] = x_s[...] + acc_s[...]


def _mix_ffn(a, b, w_out, res, gain, w_gate_up, w_down, tm=512, n_ff_chunks=2):
    m, d = res.shape
    ka = a.shape[1]
    d_ff = w_down.shape[0]
    tf = d_ff // n_ff_chunks
    assert tf * n_ff_chunks == d_ff and tf % LANES == 0
    wo = w_out.astype(BF16)
    wgu = w_gate_up.astype(BF16)
    wd = w_down.astype(BF16)
    return pl.pallas_call(
        _mix_ffn_kernel,
        grid=(m // tm, n_ff_chunks),
        in_specs=[
            pl.BlockSpec((tm, ka), lambda i, c: (i, 0)),
            pl.BlockSpec((tm, ka), lambda i, c: (i, 0)),
            pl.BlockSpec((ka, d), lambda i, c: (0, 0)),
            pl.BlockSpec((ka, d), lambda i, c: (1, 0)),
            pl.BlockSpec((tm, d), lambda i, c: (i, 0)),
            pl.BlockSpec((1, d), lambda i, c: (0, 0)),
            pl.BlockSpec((d, tf), lambda i, c: (0, c)),
            pl.BlockSpec((d, tf), lambda i, c: (0, n_ff_chunks + c)),
            pl.BlockSpec((tf, d), lambda i, c: (c, 0)),
        ],
        out_specs=pl.BlockSpec((tm, d), lambda i, c: (i, 0)),
        out_shape=jax.ShapeDtypeStruct((m, d), F32),
        scratch_shapes=[pltpu.VMEM((tm, d), F32), pltpu.VMEM((tm, d), BF16), pltpu.VMEM((tm, d), F32)],
        compiler_params=_cparams("parallel", "arbitrary"),
        name="mix_swiglu_ffn",
    )(a, b, wo, wo, res, gain.reshape(1, d), wgu, wgu, wd)


def _decay_kernel(f_ref, b_ref, c_ref, carry_s):
    @pl.when(pl.program_id(1) == 0)
    def _():
        carry_s[...] = jnp.zeros_like(carry_s)

    ts = f_ref.shape[0]
    logf = _log_sigmoid(f_ref[...] + b_ref[...])
    r = lax.broadcasted_iota(jnp.int32, (ts, ts), 0)
    c = lax.broadcasted_iota(jnp.int32, (ts, ts), 1)
    tri = (c <= r).astype(BF16)
    cs = carry_s[...]
    for term in _split3(logf):
        cs = cs + _dot(tri, term.astype(BF16))
    carry_s[...] = cs[ts - 1:ts, :]
    c_ref[...] = cs[:, 0:N_HEADS]


def _log_decay(p, col_blk, bias, b, s, ts=256):
    ns = s // ts
    bias128 = jnp.zeros((1, LANES), F32).at[0, :N_HEADS].set(bias)
    return pl.pallas_call(
        _decay_kernel,
        grid=(b, ns),
        in_specs=[
            pl.BlockSpec((ts, LANES), lambda bi, si: (bi * ns + si, col_blk)),
            pl.BlockSpec((1, LANES), lambda bi, si: (0, 0)),
        ],
        out_specs=pl.BlockSpec((ts, N_HEADS), lambda bi, si: (bi * ns + si, 0)),
        out_shape=jax.ShapeDtypeStruct((b * s, N_HEADS), F32),
        scratch_shapes=[pltpu.VMEM((1, LANES), F32)],
        compiler_params=_cparams("parallel", "arbitrary"),
        name="fox_log_decay",
    )(p, bias128)


def _attend(first_scores, first_group, scores_fn, vt_s, s_s, p_s, alive_fn=None):
    w = 2 * TQ
    s_s[0] = first_scores
    p_s[0] = jnp.zeros((KG, w), BF16)

    def values(g, pr, al, a0, a1):
        k0 = pl.multiple_of(g * KG, KG)
        a0 = al[:, 0:TQ] * a0 + _dot(vt_s[0, :, pl.ds(k0, KG)], pr[:, 0:TQ])
        a1 = al[:, TQ:w] * a1 + _dot(vt_s[1, :, pl.ds(k0, KG)], pr[:, TQ:w])
        return a0, a1

    def softmax(s, m):
        m_new = jnp.maximum(m, jnp.max(s, axis=0, keepdims=True))
        return m_new, jnp.exp(m - m_new), jnp.exp(s - m_new).astype(BF16)

    def cond(carry):
        return jnp.logical_and(carry[0] < first_group, carry[1] > 0)

    def body(carry):
        i, _, m, al, a0, a1 = carry
        par = lax.rem(i, 2)
        g = first_group - i
        pr_prev = p_s[par]
        s = s_s[par]
        a0, a1 = values(jnp.minimum(g + 1, first_group), pr_prev, al, a0, a1)
        s_next = scores_fn(g - 1)
        m_new, alpha, pr = softmax(s, m)
        s_s[1 - par] = s_next
        p_s[1 - par] = pr
        alive = jnp.int32(1) if alive_fn is None else alive_fn(g - 1, m_new).astype(jnp.int32)
        return i + 1, alive, m_new, alpha, a0, a1

    def multi(k):
        def cond_k(carry):
            return jnp.logical_and(carry[0] + (k - 1) < first_group, carry[1] > 0)

        def body_k(carry):
            i, _, m, al, a0, a1 = carry
            par = lax.rem(i, 2)
            g = first_group - i
            pr = p_s[par]
            s = s_s[par]
            g_prev = jnp.minimum(g + 1, first_group)
            for j in range(k):
                a0, a1 = values(g_prev, pr, al, a0, a1)
                s_next = scores_fn(g - j - 1)
                m, al, pr = softmax(s, m)
                g_prev, s = g - j, s_next
            out = lax.rem(i + k, 2)
            s_s[out] = s
            p_s[out] = pr
            alive = jnp.int32(1) if alive_fn is None else alive_fn(g - k, m).astype(jnp.int32)
            return i + k, alive, m, al, a0, a1

        return cond_k, body_k

    init = (jnp.int32(0), jnp.int32(1), jnp.full((1, w), -jnp.inf, F32), jnp.ones((1, w), F32),
            jnp.zeros((VT_ROWS, TQ), F32), jnp.zeros((VT_ROWS, TQ), F32))
    carry = init
    for k in (3, 2):
        carry = lax.while_loop(*multi(k), carry)
    n, _, m, al, a0, a1 = lax.while_loop(cond, body, carry)
    par = lax.rem(n, 2)
    g = first_group - n
    a0, a1 = values(jnp.minimum(g + 1, first_group), p_s[par], al, a0, a1)
    _, alpha, pr = softmax(s_s[par], m)
    a0, a1 = values(g, pr, alpha, a0, a1)
    return (a0[0:HEAD_DIM, :] / a0[HEAD_DIM:HEAD_DIM + 1, :],
            a1[0:HEAD_DIM, :] / a1[HEAD_DIM:HEAD_DIM + 1, :])


def _score_scratch():
    return [pltpu.VMEM((2, KG, 2 * TQ), F32), pltpu.VMEM((2, KG, 2 * TQ), BF16)]


def _key_query_iotas(nk, nq):
    k = lax.broadcasted_iota(jnp.int32, (nk, nq), 0)
    q = lax.broadcasted_iota(jnp.int32, (nk, nq), 1)
    return k, q


def _pair_causal(nk, tq, shift=0, strict=False):
    kidx, qidx = _key_query_iotas(nk, 2 * tq)
    qidx = jnp.where(qidx >= tq, qidx - tq, qidx) + shift
    return kidx < qidx if strict else kidx <= qidx


def _values_with_ones(vt):
    return jnp.where(_row_iota() < HEAD_DIM, vt, 1.0)[0:VT_ROWS, :]


def _prep_loop(n_rows, fn):
    def body(i, carry):
        fn(pl.multiple_of(i * PREP_ROWS, PREP_ROWS))
        return carry

    lax.fori_loop(0, n_rows // PREP_ROWS, body, 0)


def _pair_specs(s, qcol, kcol, vcol):
    return [pl.BlockSpec((s, LANES), lambda b, hp, qi, col=col: (b, col + hp)) for col in (qcol, kcol, vcol)]


def _diag_group(qi):
    g0 = (qi * TQ + TQ - 1) // KG
    return g0, qi * TQ - g0 * KG


def _store_pair(o_ref, out0, out1):
    o_ref[...] = jnp.concatenate([out0, out1], axis=0).T.astype(o_ref.dtype)


def _fox_kernel(q_ref, k_ref, v_ref, c_ref, gq_ref, gk_ref, o_ref,
                qt_s, cq_s, ka_s, vt_s, s_s, p_s, *, seq):
    hp = pl.program_id(1)
    qi = pl.program_id(2)
    lane = _lane_iota()
    first = lane < HEAD_DIM
    head_lane = lax.broadcasted_iota(jnp.int32, (1, N_HEADS), 1)
    second = lax.broadcasted_iota(jnp.int32, (1, 2 * TQ), 1) >= TQ

    def augment(x, c, key_side):
        hi, mid, lo = _split3(c)
        sgn = -1.0 if key_side else 1.0
        c0 = HEAD_DIM + (3 if key_side else 0)
        one0 = HEAD_DIM + (0 if key_side else 3)
        out = jnp.where(first, x, 0.0)
        out = jnp.where(jnp.logical_and(lane >= one0, lane < one0 + 3), 1.0, out)
        for i, t in enumerate((hi, mid, lo)):
            out = jnp.where(lane == c0 + i, sgn * t, out)
        return out

    def head_col(cc, h):
        return jnp.sum(jnp.where(head_lane == h, cc, 0.0), axis=1, keepdims=True)

    @pl.when(qi == 0)
    def _():
        def prep(r0):
            rs = pl.ds(r0, PREP_ROWS)
            kn = _pair_rms(k_ref[rs, :], gk_ref[...])
            qn = _pair_rms(q_ref[rs, :], gq_ref[...]) * (HEAD_DIM ** -0.5)
            vt = v_ref[rs, :].T
            cc = c_ref[rs, :]
            for j in range(2):
                c = head_col(cc, 2 * hp + j)
                kj = kn if j == 0 else pltpu.roll(kn, HEAD_DIM, 1)
                ka_s[j, rs, :] = augment(kj, c, True).astype(BF16)
                vj = vt if j == 0 else pltpu.roll(vt, HEAD_DIM, 0)
                vt_s[j, :, rs] = _values_with_ones(vj).astype(BF16)
                qa = augment(qn if j == 0 else pltpu.roll(qn, HEAD_DIM, 1), c, False).T
                qt_s[j, :, rs] = qa.astype(BF16)
                cq_s[j, 0:1, rs] = jnp.sum(qa[HEAD_DIM:HEAD_DIM + 3, :], axis=0, keepdims=True)

        _prep_loop(seq, prep)

    qs = pl.ds(pl.multiple_of(qi * TQ, TQ), TQ)
    qt = [qt_s[0, :, qs], qt_s[1, :, qs]]
    qk_bound = 1.02 * HEAD_DIM ** 0.5 * (jnp.max(jnp.abs(gq_ref[...]), axis=1, keepdims=True)
                                          * jnp.max(jnp.abs(gk_ref[...]), axis=1, keepdims=True))
    score_bound = qk_bound + jnp.concatenate([cq_s[0, 0:1, qs], cq_s[1, 0:1, qs]], axis=1)

    def scores(g):
        ks = pl.ds(pl.multiple_of(g * KG, KG), KG)
        return jnp.concatenate([_dot(ka_s[0, ks, :], qt[0]), _dot(ka_s[1, ks, :], qt[1])], axis=1)

    def alive(g, m):
        row = c_ref[pl.ds((g + 1) * KG - 1, 1), :]
        c0 = jnp.sum(jnp.where(head_lane == 2 * hp, row, 0.0), axis=1, keepdims=True)
        c1 = jnp.sum(jnp.where(head_lane == 2 * hp + 1, row, 0.0), axis=1, keepdims=True)
        return jnp.max(score_bound - jnp.where(second, c1, c0) - m) > DEAD_LOG

    g0, shift = _diag_group(qi)
    diag = _pair_causal(KG, TQ, shift)
    _store_pair(o_ref, *_attend(jnp.where(diag, scores(g0), -jnp.inf), g0, scores, vt_s, s_s, p_s, alive))


def _fox_attention(p, cdecay, gq, gk, b, s):
    nq = s // TQ
    gq2 = jnp.tile(gq, 2).reshape(1, LANES)
    gk2 = jnp.tile(gk, 2).reshape(1, LANES)
    return pl.pallas_call(
        functools.partial(_fox_kernel, seq=s),
        grid=(b, N_PAIRS, nq),
        in_specs=_pair_specs(s, 0, N_PAIRS, 2 * N_PAIRS) + [
            pl.BlockSpec((s, N_HEADS), lambda b_, hp, qi: (b_, 0)),
            pl.BlockSpec((1, LANES), lambda b_, hp, qi: (0, 0)),
            pl.BlockSpec((1, LANES), lambda b_, hp, qi: (0, 0)),
        ],
        out_specs=pl.BlockSpec((TQ, LANES), lambda b_, hp, qi: (b_ * nq + qi, hp)),
        out_shape=jax.ShapeDtypeStruct((b * s, N_PAIRS * LANES), BF16),
        scratch_shapes=[pltpu.VMEM((2, LANES, s), BF16), pltpu.VMEM((2, 8, s), F32),
                        pltpu.VMEM((2, s, LANES), BF16), pltpu.VMEM((2, VT_ROWS, s), BF16)] + _score_scratch(),
        compiler_params=_attention_cparams(),
        name="fox_attention",
    )(p, p, p, cdecay, gq2, gk2)


def _sb_kernel(q_ref, k_ref, v_ref, u_ref, o_ref, qt_s, kb_s, vt_s, *, seq):
    qi = pl.program_id(2)
    first = _lane_iota() < HEAD_DIM
    w = 2 * SB_TQ

    @pl.when(qi == 0)
    def _():
        def prep(r0):
            rs = pl.ds(r0, PREP_ROWS)
            kb_s[rs, :] = k_ref[rs, :].astype(BF16)
            vt_s[:, rs] = v_ref[rs, :].T.astype(BF16)
            q = q_ref[rs, :] * (HEAD_DIM ** -0.5)
            qt_s[0, :, rs] = jnp.where(first, q, 0.0).T.astype(BF16)
            qt_s[1, :, rs] = jnp.where(first, 0.0, q).T.astype(BF16)

        _prep_loop(seq, prep)

    qs = pl.ds(pl.multiple_of(qi * SB_TQ, SB_TQ), SB_TQ)
    qt = jnp.concatenate([qt_s[0, :, qs], qt_s[1, :, qs]], axis=1)

    def chunk(k0, nk, drop, acc, strict):
        z = _dot(kb_s[pl.ds(k0, nk), :], qt)
        softplus = jnp.maximum(z, 0.0) + jnp.log(1.0 + jnp.exp(-jnp.abs(z)))
        log_b = z - softplus
        if strict is not None:
            softplus = jnp.where(strict, softplus, 0.0)
        hi = softplus.astype(BF16)
        lo = (softplus - hi.astype(F32)).astype(BF16)
        later = u_ref[0:nk, 0:nk]
        wgt = jnp.exp(log_b - (drop + (_dot(later, hi) + _dot(later, lo))))
        if strict is not None:
            wgt = jnp.where(strict, wgt, 0.0)
        acc = acc + _dot(vt_s[:, pl.ds(k0, nk)], wgt.astype(BF16))
        return drop + jnp.sum(softplus, axis=0, keepdims=True), acc

    k0 = pl.multiple_of(jnp.maximum(qi - 1, 0) * TK, TK)
    strict = _pair_causal(2 * TK, SB_TQ, qi * SB_TQ - k0, strict=True)
    rsum, acc = chunk(k0, 2 * TK, jnp.zeros((1, w), F32), jnp.zeros((LANES, w), F32), strict)

    def cond(c):
        return jnp.logical_and(c[0] >= 0, c[1] > 0)

    def alive(drop):
        return (jnp.min(drop) < -DEAD_LOG).astype(jnp.int32)

    def body(c):
        rs, ac = chunk(pl.multiple_of(c[0] * TK, TK), TK, c[2], c[3], None)
        return c[0] - 1, alive(rs), rs, ac

    _, _, _, acc = lax.while_loop(cond, body, (qi - 2, alive(rsum), rsum, acc))
    _store_pair(o_ref, acc[0:HEAD_DIM, 0:SB_TQ], acc[HEAD_DIM:LANES, SB_TQ:w])


def _sb_attention(p, qcol, b, s):
    assert SB_TQ == TK
    nq = s // SB_TQ
    r = lax.broadcasted_iota(jnp.int32, (2 * TK, 2 * TK), 0)
    c = lax.broadcasted_iota(jnp.int32, (2 * TK, 2 * TK), 1)
    later = (c > r).astype(BF16)
    return pl.pallas_call(
        functools.partial(_sb_kernel, seq=s),
        grid=(b, N_PAIRS, nq),
        in_specs=_pair_specs(s, qcol, qcol + N_PAIRS, qcol + 2 * N_PAIRS) + [
            pl.BlockSpec((2 * TK, 2 * TK), lambda b_, hp, qi: (0, 0)),
        ],
        out_specs=pl.BlockSpec((SB_TQ, LANES), lambda b_, hp, qi: (b_ * nq + qi, hp)),
        out_shape=jax.ShapeDtypeStruct((b * s, N_PAIRS * LANES), BF16),
        scratch_shapes=[pltpu.VMEM((2, LANES, s), BF16), pltpu.VMEM((s, LANES), BF16), pltpu.VMEM((LANES, s), BF16)],
        compiler_params=_attention_cparams(),
        name="stick_breaking_attention",
    )(p, p, p, later)


def _t5_bucket(rel):
    n = jnp.maximum(rel, 0)
    nf = jnp.maximum(n, 1).astype(F32)
    large = REL_MAX_EXACT + (jnp.log(nf / REL_MAX_EXACT) / math.log(REL_MAX_DISTANCE / REL_MAX_EXACT)
                             * (REL_BUCKETS - REL_MAX_EXACT)).astype(jnp.int32)
    large = jnp.minimum(large, REL_BUCKETS - 1)
    return jnp.where(n < REL_MAX_EXACT, n, large)


def _bias_tile_kernel(rb_ref, own_ref, prev_ref):
    hp = pl.program_id(0)
    kidx, qidx = _key_query_iotas(MOBA_BLOCK, MOBA_BLOCK)
    for off, ref in ((0, own_ref), (MOBA_BLOCK, prev_ref)):
        bucket = _t5_bucket(qidx - kidx + off)
        for j in range(2):
            out = jnp.zeros((MOBA_BLOCK, MOBA_BLOCK), F32)
            for bk in range(REL_BUCKETS):
                out = jnp.where(bucket == bk, rb_ref[2 * hp + j, bk], out)
            ref[0, :, j * MOBA_BLOCK:(j + 1) * MOBA_BLOCK] = out - rb_ref[2 * hp + j, REL_BUCKETS - 1]


def _bias_tiles(rel_bias):
    shape = jax.ShapeDtypeStruct((N_PAIRS, MOBA_BLOCK, 2 * MOBA_BLOCK), F32)
    spec = pl.BlockSpec((1, MOBA_BLOCK, 2 * MOBA_BLOCK), lambda h: (h, 0, 0))
    return pl.pallas_call(
        _bias_tile_kernel,
        grid=(N_PAIRS,),
        in_specs=[pl.BlockSpec(memory_space=pltpu.SMEM)],
        out_specs=[spec, spec],
        out_shape=[shape, shape],
        compiler_params=_cparams("arbitrary"),
        name="moba_bias_tiles",
    )(rel_bias)


def _moba_kernel(q_ref, k_ref, v_ref, bown_ref, bprev_ref, gq_ref, gk_ref, o_ref,
                 qt_s, kn_s, vt_s, km_s, sel_s, s_s, p_s, *, seq):
    qi = pl.program_id(2)
    nb = seq // MOBA_BLOCK
    first = _lane_iota() < HEAD_DIM
    w = 2 * TQ
    per_step = TQ // MOBA_BLOCK
    per_group = KG // MOBA_BLOCK

    @pl.when(qi == 0)
    def _():
        def prep_keys(n, carry):
            rs = pl.ds(pl.multiple_of(n * MOBA_BLOCK, MOBA_BLOCK), MOBA_BLOCK)
            kn = _pair_rms(k_ref[rs, :], gk_ref[...])
            kn_s[rs, :] = kn.astype(BF16)
            km_s[pl.ds(n, 1), :] = jnp.mean(kn, axis=0, keepdims=True)
            vt = _transposed(v_ref[rs, :].astype(BF16))
            vt_s[0, :, rs] = _values_with_ones(vt).astype(BF16)
            vt_s[1, :, rs] = _values_with_ones(pltpu.roll(vt, HEAD_DIM, 0)).astype(BF16)
            return carry

        lax.fori_loop(0, nb, prep_keys, 0)

        def prep_queries(m, carry):
            blk = lax.broadcasted_iota(jnp.int32, (nb, 1), 0)
            blocks = [2 * m, 2 * m + 1]
            rows = [pl.ds(pl.multiple_of(n * MOBA_BLOCK, MOBA_BLOCK), MOBA_BLOCK) for n in blocks]
            qhs = []
            for rs in rows:
                qn = _pair_rms(q_ref[rs, :], gq_ref[...]) * (HEAD_DIM ** -0.5)
                qhs.append((jnp.where(first, qn, 0.0), jnp.where(first, 0.0, qn)))
            nt = (((1,), (1,)), ((), ()))
            gates = [jnp.concatenate([lax.dot_general(km_s[...], qh, nt, precision=HIGHEST, preferred_element_type=F32)
                                      for qh in qh2], axis=1) for qh2 in qhs]
            for n, rs, qh2, gate in zip(blocks, rows, qhs, gates):
                past = blk < n
                gate = jnp.where(past, gate, -jnp.inf)
                beaten = jnp.zeros((nb, 2 * MOBA_BLOCK), F32)
                for n2 in range(nb):
                    g2 = gate[n2:n2 + 1, :]
                    wins = jnp.logical_or(g2 > gate, jnp.logical_and(g2 == gate, n2 < blk))
                    beaten = beaten + jnp.where(wins, 1.0, 0.0)
                sel = jnp.where(jnp.logical_and(beaten < min(MOBA_TOPK, nb), past), 1.0, 0.0)
                for j in range(2):
                    qt_s[j, :, rs] = _transposed(qh2[j].astype(BF16)).astype(BF16)
                    sel_s[j, :, rs] = sel[:, j * MOBA_BLOCK:(j + 1) * MOBA_BLOCK]
            return carry

        lax.fori_loop(0, nb // 2, prep_queries, 0)

    qs = pl.ds(pl.multiple_of(qi * TQ, TQ), TQ)
    qt = jnp.concatenate([qt_s[0, :, qs], qt_s[1, :, qs]], axis=1)
    lane_w = lax.broadcasted_iota(jnp.int32, (1, w), 1)
    in_step = jnp.where(lane_w >= TQ, lane_w - TQ, lane_w)
    cur = qi * per_step + jnp.right_shift(in_step, MOBA_BLOCK.bit_length() - 1)

    def per_query_block(ref):
        return jnp.concatenate([ref[0, :, j * MOBA_BLOCK:(j + 1) * MOBA_BLOCK]
                                for j in range(2) for _ in range(per_step)], axis=1)

    def group_scores(g, diagonal):
        raw = _dot(kn_s[pl.ds(pl.multiple_of(g * KG, KG), KG), :], qt)
        pieces = []
        for i in range(per_group):
            n = g * per_group + i
            s = raw[i * MOBA_BLOCK:(i + 1) * MOBA_BLOCK, :]
            mask = jnp.concatenate([sel_s[0, pl.ds(n, 1), qs], sel_s[1, pl.ds(n, 1), qs]], axis=1) > 0.5
            follows = cur == n + 1
            if diagonal:
                own = cur == n
                kidx, qidx = _key_query_iotas(MOBA_BLOCK, w)
                causal = kidx <= jnp.bitwise_and(qidx, MOBA_BLOCK - 1)
                mask = jnp.logical_or(jnp.logical_and(own, causal), jnp.logical_and(jnp.logical_not(own), mask))
                s = s + jnp.where(own, per_query_block(bown_ref), jnp.where(follows, per_query_block(bprev_ref), 0.0))
            elif i == per_group - 1:
                s = s + jnp.where(follows, per_query_block(bprev_ref), 0.0)
            pieces.append(jnp.where(mask, s, -jnp.inf))
        return jnp.concatenate(pieces, axis=0)

    g0, _ = _diag_group(qi)
    _store_pair(o_ref, *_attend(group_scores(g0, True), g0, lambda g: group_scores(g, False), vt_s, s_s, p_s))


def _moba_attention(p, bown, bprev, gq, gk, b, s):
    assert KG % TQ == 0 and TQ % MOBA_BLOCK == 0 and s % KG == 0
    nq = s // TQ
    nb = s // MOBA_BLOCK
    gq2 = jnp.tile(gq, 2).reshape(1, LANES)
    gk2 = jnp.tile(gk, 2).reshape(1, LANES)
    bias_spec = pl.BlockSpec((1, MOBA_BLOCK, 2 * MOBA_BLOCK), lambda b_, hp, qi: (hp, 0, 0))
    return pl.pallas_call(
        functools.partial(_moba_kernel, seq=s),
        grid=(b, N_PAIRS, nq),
        in_specs=_pair_specs(s, 0, N_PAIRS, 2 * N_PAIRS) + [
            bias_spec, bias_spec,
            pl.BlockSpec((1, LANES), lambda b_, hp, qi: (0, 0)),
            pl.BlockSpec((1, LANES), lambda b_, hp, qi: (0, 0)),
        ],
        out_specs=pl.BlockSpec((TQ, LANES), lambda b_, hp, qi: (b_ * nq + qi, hp)),
        out_shape=jax.ShapeDtypeStruct((b * s, N_PAIRS * LANES), BF16),
        scratch_shapes=[pltpu.VMEM((2, LANES, s), BF16), pltpu.VMEM((s, LANES), BF16),
                        pltpu.VMEM((2, VT_ROWS, s), BF16), pltpu.VMEM((nb, LANES), F32),
                        pltpu.VMEM((2, nb, s), F32)] + _score_scratch(),
        compiler_params=_attention_cparams(),
        name="moba_attention",
    )(p, p, p, bown, bprev, gq2, gk2)


def _rope(x, cos, sin):
    return x * cos + pltpu.roll(x, LANES - MLA_ROPE // 2, 1) * sin


def _mla_norm_rope(x, gain, cos, sin):
    ms = jnp.sum(jnp.where(_lane_iota() < MLA_QK, x * x, 0.0), axis=1, keepdims=True) * (1.0 / MLA_QK)
    return _rope(x * lax.rsqrt(ms + RMS_EPS) * gain, cos, sin)


def _mla_lanes(t):
    x1 = t[..., MLA_NOPE:MLA_NOPE + MLA_ROPE // 2]
    pad = jnp.zeros(t.shape[:-1] + (LANES - MLA_QK - MLA_ROPE // 2,), t.dtype)
    return jnp.concatenate([t, x1, pad], axis=-1)


def _mla_kernel(q_ref, kv_ref, kr_ref, cos_ref, sin_ref, gq_ref, gk_ref, o_ref, qt_s, kn_s, vt_s, s_s, p_s, *, seq):
    qi = pl.program_id(2)
    first = _lane_iota() < HEAD_DIM

    @pl.when(qi == 0)
    def _():
        def prep(r0):
            rs = pl.ds(r0, PREP_ROWS)
            cos, sin = cos_ref[rs, :], sin_ref[rs, :]
            for j in range(2):
                kv = kv_ref[rs, j * LANES:(j + 1) * LANES]
                k = jnp.where(first, kv, kr_ref[rs, :])
                kn_s[j, rs, :] = _mla_norm_rope(k, gk_ref[...], cos, sin).astype(BF16)
                vt_s[j, :, rs] = _values_with_ones(pltpu.roll(_transposed(kv.astype(BF16)), HEAD_DIM, 0)).astype(BF16)
                q = _mla_norm_rope(q_ref[rs, j * LANES:(j + 1) * LANES], gq_ref[...], cos, sin) * (MLA_QK ** -0.5)
                qt_s[j, :, rs] = _transposed(q.astype(BF16)).astype(BF16)

        _prep_loop(seq, prep)

    qs = pl.ds(pl.multiple_of(qi * TQ, TQ), TQ)
    qt = [qt_s[0, :, qs], qt_s[1, :, qs]]

    def scores(g):
        ks = pl.ds(pl.multiple_of(g * KG, KG), KG)
        return jnp.concatenate([_dot(kn_s[0, ks, :], qt[0]), _dot(kn_s[1, ks, :], qt[1])], axis=1)

    g0, shift = _diag_group(qi)
    diag = _pair_causal(KG, TQ, shift)
    _store_pair(o_ref, *_attend(jnp.where(diag, scores(g0), -jnp.inf), g0, scores, vt_s, s_s, p_s))


def _mla_attention(qd, kvd, p, kr_col, tables, gq, gk, b, s):
    nq = s // TQ
    cos, sin = tables
    gq128 = _mla_lanes(gq).reshape(1, LANES)
    gk128 = _mla_lanes(gk).reshape(1, LANES)
    full = pl.BlockSpec((s, LANES), lambda b_, hp, qi: (0, 0))
    gain = pl.BlockSpec((1, LANES), lambda b_, hp, qi: (0, 0))
    return pl.pallas_call(
        functools.partial(_mla_kernel, seq=s),
        grid=(b, N_PAIRS, nq),
        in_specs=[
            pl.BlockSpec((s, 2 * LANES), lambda b_, hp, qi: (b_, hp)),
            pl.BlockSpec((s, 2 * LANES), lambda b_, hp, qi: (b_, hp)),
            pl.BlockSpec((s, LANES), lambda b_, hp, qi: (b_, kr_col)),
            full, full, gain, gain,
        ],
        out_specs=pl.BlockSpec((TQ, LANES), lambda b_, hp, qi: (b_ * nq + qi, hp)),
        out_shape=jax.ShapeDtypeStruct((b * s, N_PAIRS * LANES), BF16),
        scratch_shapes=[pltpu.VMEM((2, LANES, s), BF16), pltpu.VMEM((2, s, LANES), BF16),
                        pltpu.VMEM((2, VT_ROWS, s), BF16)] + _score_scratch(),
        compiler_params=_attention_cparams(),
        name="mla_attention",
    )(qd, kvd, p, cos, sin, gq128, gk128)


def _rope_tables(s):
    half = MLA_ROPE // 2
    inv_freq = ROPE_BASE ** (-jnp.arange(half, dtype=F32) / half)
    ang = jnp.arange(s, dtype=F32)[:, None] * inv_freq[None, :]
    cos, sin = jnp.cos(ang), jnp.sin(ang)
    z = lambda w: jnp.zeros((s, w), F32)
    cos_t = jnp.concatenate([jnp.ones((s, MLA_NOPE), F32), cos, cos, z(LANES - MLA_QK)], axis=1)
    sin_t = jnp.concatenate([z(MLA_NOPE), -sin, sin, z(LANES - MLA_QK)], axis=1)
    return cos_t, sin_t


def _even_mixer(x, b, s, norm, w_in, forget_bias, gq, gk):
    d = x.shape[1]
    w = HEAD_DIM * N_HEADS
    cuts = [0, w, 2 * w, 3 * w, 3 * w + N_HEADS, 4 * w + N_HEADS, 5 * w + N_HEADS, 6 * w + N_HEADS]
    qa, ka, va, fa, qb, kb, vb = (w_in[:, cuts[i]:cuts[i + 1]] for i in range(7))
    w_perm = jnp.concatenate([qa, ka, va, qb, kb, vb, fa, jnp.zeros((d, LANES - N_HEADS), F32)], axis=1).astype(BF16)
    p = _norm_matmul(x, 0, d, norm, w_perm)
    cdecay = _log_decay(p, 6 * N_PAIRS, forget_bias, b, s)
    out_a = _fox_attention(p, cdecay, gq, gk, b, s)
    out_b = _sb_attention(p, 3 * N_PAIRS, b, s)
    return out_a, out_b


def _odd_mixer(x, b, s, norm, w_in, gq_moba, gk_moba, q_a_norm, w_q_b, kv_a_norm, w_kv_b, gq_mla, gk_mla,
               bias_tiles, rope_tables):
    d = x.shape[1]
    n_main = 3 * HEAD_DIM * N_HEADS + MLA_Q_LORA + MLA_KV_LORA
    w_rope = _mla_lanes(jnp.concatenate([jnp.zeros((d, MLA_NOPE), F32), w_in[:, n_main:]], axis=1))
    w_perm = jnp.concatenate([w_in[:, :n_main], w_rope], axis=1).astype(BF16)
    p = _norm_matmul(x, 0, d, norm, w_perm)
    out_c = _moba_attention(p, bias_tiles[0], bias_tiles[1], gq_moba, gk_moba, b, s)
    lat0 = 3 * HEAD_DIM * N_HEADS
    w_q = _mla_lanes(w_q_b.reshape(MLA_Q_LORA, N_HEADS, MLA_QK))
    qd = _norm_matmul(p, lat0 // MLA_Q_LORA, MLA_Q_LORA, q_a_norm, w_q.reshape(MLA_Q_LORA, N_HEADS * LANES).astype(BF16))
    kvd = _norm_matmul(p, (lat0 + MLA_Q_LORA) // MLA_KV_LORA, MLA_KV_LORA, kv_a_norm, w_kv_b.astype(BF16))
    out_d = _mla_attention(qd, kvd, p, (n_main // LANES), rope_tables, gq_mla, gk_mla, b, s)
    return out_c, out_d


def kernel(x, ffn_norm, ffn_w_gate_up, ffn_w_down, rel_bias, ev_norm, ev_w_in, ev_forget_bias, ev_fox_q_norm, ev_fox_k_norm, ev_w_out, od_norm, od_w_in, od_moba_q_norm, od_moba_k_norm, od_mla_q_a_norm, od_mla_w_q_b, od_mla_kv_a_norm, od_mla_w_kv_b, od_mla_q_norm, od_mla_k_norm, od_w_out):
    b, s, d = x.shape
    depth = ffn_norm.shape[0]
    bias_tiles = _bias_tiles(rel_bias)
    rope_tables = _rope_tables(s)
    x = x.reshape(b * s, d)
    for layer in range(depth):
        i = layer // 2
        if layer % 2 == 0:
            mixed = _even_mixer(x, b, s, ev_norm[i], ev_w_in[i], ev_forget_bias[i], ev_fox_q_norm[i],
                                ev_fox_k_norm[i])
            w_out = ev_w_out[i]
        else:
            mixed = _odd_mixer(x, b, s, od_norm[i], od_w_in[i], od_moba_q_norm[i], od_moba_k_norm[i],
                               od_mla_q_a_norm[i], od_mla_w_q_b[i], od_mla_kv_a_norm[i], od_mla_w_kv_b[i],
                               od_mla_q_norm[i], od_mla_k_norm[i], bias_tiles, rope_tables)
            w_out = od_w_out[i]
        x = _mix_ffn(mixed[0], mixed[1], w_out, x, ffn_norm[layer], ffn_w_gate_up[layer], ffn_w_down[layer])
    return x.reshape(b, s, d)
```
